```python
import math
import jax, jax.numpy as jnp
from jax import lax
import numpy as np

D_MODEL = 1024
BATCH = 8
SEQ = 4096
DEPTH = 1

HEAD_DIM = 64
HEADS_PER_GROUP = 8
ATTN_GROUPS = ((128, 1), (512, 4), (2048, 16))
N_GROUPS = len(ATTN_GROUPS)
N_ATTN_HEADS = N_GROUPS * HEADS_PER_GROUP
ATTN_WIDTH = HEADS_PER_GROUP * HEAD_DIM
QKV_WIDTH = N_GROUPS * 3 * ATTN_WIDTH
BLOCK = 128
POOL_WINDOWS = (2, 4, 8, 16)
POOL_GROUPS = len(POOL_WINDOWS)
POOL_WIDTH = D_MODEL // 2
PGW = POOL_WIDTH // POOL_GROUPS
NUM_BUCKETS = 32
MAX_DISTANCE = 2048
EPS = 1e-6
SPLIT_SIZES = (QKV_WIDTH, ATTN_WIDTH, POOL_WIDTH, POOL_WIDTH, D_MODEL, D_MODEL)
SPLIT_POINTS = (QKV_WIDTH,
                QKV_WIDTH + ATTN_WIDTH,
                QKV_WIDTH + ATTN_WIDTH + POOL_WIDTH,
                QKV_WIDTH + ATTN_WIDTH + 2 * POOL_WIDTH,
                QKV_WIDTH + ATTN_WIDTH + 2 * POOL_WIDTH + D_MODEL)
IN_WIDTH = QKV_WIDTH + ATTN_WIDTH + 2 * POOL_WIDTH + 2 * D_MODEL

kernel_name = "hybrid_dilated_attn_pool_gated_block"


def rmsnorm(x, g):
    xf = x.astype(jnp.float32)
    y = xf * lax.rsqrt(jnp.mean(xf * xf, axis=-1, keepdims=True) + EPS)
    return (y * g.astype(jnp.float32)).astype(x.dtype)


def t5_bucket(n):
    max_exact = NUM_BUCKETS // 2
    nf = jnp.maximum(n, 1).astype(jnp.float32)
    large = max_exact + (jnp.log(nf / max_exact) / math.log(MAX_DISTANCE / max_exact)
                         * (NUM_BUCKETS - max_exact)).astype(jnp.int32)
    large = jnp.minimum(large, NUM_BUCKETS - 1)
    return jnp.where(n < max_exact, n, large)


def to_sub(t, dil):
    B, S = t.shape[:2]
    L = S // dil
    t = t.reshape((B, L, dil) + t.shape[2:])
    t = jnp.moveaxis(t, 2, 1)
    return t.reshape((B * dil, L) + t.shape[3:])


def from_sub(t, B, dil):
    L = t.shape[1]
    t = t.reshape((B, dil, L) + t.shape[2:])
    t = jnp.moveaxis(t, 1, 2)
    return t.reshape((B, L * dil) + t.shape[3:])


def dilated_window_attention(q, k, v, dil, n_back, bias_g):
    B, S, H, Dh = q.shape
    L = S // dil
    nb = -(-L // BLOCK)
    pad = nb * BLOCK - L
    Bd = B * dil

    def sub(t):
        return jnp.pad(to_sub(t, dil), ((0, 0), (0, pad), (0, 0), (0, 0)))

    def band(t):
        tp = jnp.pad(t, ((0, 0), (BLOCK, 0), (0, 0), (0, 0)))
        prev = tp[:, :-BLOCK].reshape(Bd, nb, BLOCK, H, Dh)
        cur = t.reshape(Bd, nb, BLOCK, H, Dh)
        return jnp.concatenate([prev, cur], axis=2)

    qb = sub(q).reshape(Bd, nb, BLOCK, H, Dh)
    kb = band(sub(k))
    vb = band(sub(v))

    i = jnp.arange(BLOCK)[:, None]
    j = jnp.arange(2 * BLOCK)[None, :]
    dist = BLOCK + i - j
    band_ok = (dist >= 0) & (dist <= n_back)
    key_ok = (jnp.arange(nb)[:, None, None] * BLOCK - BLOCK + j[None]) >= 0
    mask = band_ok[None] & key_ok
    bucket = t5_bucket(jnp.clip(dist, 0, n_back) * dil)
    bias = jnp.transpose(bias_g[bucket].astype(jnp.float32), (2, 0, 1))

    logits = jnp.einsum('znqhd,znkhd->znhqk', qb, kb).astype(jnp.float32) * (HEAD_DIM ** -0.5)
    logits = jnp.where(mask[None, :, None], logits + bias[None, None], -jnp.inf)
    m = jnp.max(logits, axis=-1, keepdims=True)
    p = jnp.exp(logits - m)
    denom = jnp.sum(p, axis=-1)
    o = jnp.einsum('znhqk,znkhd->znqhd', p.astype(vb.dtype), vb).astype(jnp.float32)
    o = o / jnp.moveaxis(denom, 2, 3)[..., None]
    lse = jnp.moveaxis(m[..., 0] + jnp.log(denom), 2, 3)

    o = o.reshape(Bd, nb * BLOCK, H, Dh)[:, :L]
    lse = lse.reshape(Bd, nb * BLOCK, H)[:, :L]
    return from_sub(o, B, dil), from_sub(lse, B, dil)


def multiscale_pool(u):
    B, S, C = u.shape
    uf = u.astype(jnp.float32)
    csp = jnp.pad(jnp.cumsum(uf, axis=1), ((0, 0), (1, 0), (0, 0)))
    t = jnp.arange(S)
    outs = []
    for g, win in enumerate(POOL_WINDOWS):
        cg = csp[:, :, g * PGW:(g + 1) * PGW]
        lo = jnp.maximum(t + 1 - win, 0)
        s = cg[:, 1:] - cg[:, lo]
        cnt = jnp.minimum(t + 1, win).astype(jnp.float32)
        outs.append(s / cnt[None, :, None])
    return jnp.concatenate(outs, axis=-1) - uf


def setup_inputs(seed: int = 0) -> dict:
    key = jax.random.key(seed)
    ks = jax.random.split(key, 14)
    f32 = jnp.float32
    nrm = lambda k, shape, s: (jax.random.normal(k, shape, f32) * s).astype(f32)
    return {
        "x": nrm(ks[0], (BATCH, SEQ, D_MODEL), 1.0),
        "c": nrm(ks[1], (BATCH, D_MODEL), 1.0),
        "norm_g": 1.0 + nrm(ks[2], (DEPTH, D_MODEL), 0.05),
        "w_ada": nrm(ks[3], (DEPTH, D_MODEL, 3 * D_MODEL), 0.5 * D_MODEL ** -0.5),
        "b_ada": nrm(ks[4], (DEPTH, 3 * D_MODEL), 0.01),
        "w_in": nrm(ks[5], (DEPTH, D_MODEL, IN_WIDTH), D_MODEL ** -0.5),
        "pool_w": nrm(ks[6], (DEPTH, POOL_GROUPS, PGW, PGW), PGW ** -0.5),
        "pool_scale": 1.0 + nrm(ks[7], (DEPTH, POOL_WIDTH), 0.1),
        "w_attn_br": nrm(ks[8], (DEPTH, ATTN_WIDTH, D_MODEL), ATTN_WIDTH ** -0.5),
        "w_pool_br": nrm(ks[9], (DEPTH, POOL_WIDTH, D_MODEL), POOL_WIDTH ** -0.5),
        "w_out": nrm(ks[10], (DEPTH, D_MODEL, D_MODEL), D_MODEL ** -0.5),
        "rel_bias": nrm(ks[11], (NUM_BUCKETS, N_ATTN_HEADS), 0.5),
        "final_g": 1.0 + nrm(ks[12], (D_MODEL,), 0.05),
    }


def reference(x, c, norm_g, w_ada, b_ada, w_in, pool_w, pool_scale, w_attn_br, w_pool_br, w_out, rel_bias, final_g):
    B, S, D = x.shape
    for l in range(DEPTH):
        mod = c @ w_ada[l] + b_ada[l]
        shift, scale, gate = jnp.split(mod, 3, axis=-1)
        h = rmsnorm(x, norm_g[l]) * (1.0 + scale[:, None]) + shift[:, None]

        proj = h @ w_in[l]
        qkv, z_attn, u_pool, z_pool, g_attn, g_pool = jnp.split(proj, SPLIT_POINTS, axis=-1)
        qkv = qkv.reshape(B, S, N_GROUPS, 3, HEADS_PER_GROUP, HEAD_DIM)

        outs, lses = [], []
        for gi, (win, dil) in enumerate(ATTN_GROUPS):
            bias_g = rel_bias[:, gi * HEADS_PER_GROUP:(gi + 1) * HEADS_PER_GROUP]
            o, lse = dilated_window_attention(qkv[:, :, gi, 0], qkv[:, :, gi, 1], qkv[:, :, gi, 2],
                                              dil, win // dil, bias_g)
            outs.append(o)
            lses.append(lse)
        wts = jax.nn.softmax(jnp.stack(lses, axis=0), axis=0)
        attn = jnp.sum(wts[..., None] * jnp.stack(outs, axis=0), axis=0)
        attn = attn.reshape(B, S, ATTN_WIDTH).astype(x.dtype)
        y_attn = (attn * jax.nn.silu(z_attn)) @ w_attn_br[l]

        pooled = multiscale_pool(u_pool).reshape(B, S, POOL_GROUPS, PGW)
        mixed = jnp.einsum('bsgc,gce->bsge', pooled, pool_w[l].astype(jnp.float32))
        mixed = (mixed.reshape(B, S, POOL_WIDTH) * pool_scale[l]).astype(x.dtype)
        y_pool = (mixed * jax.nn.silu(z_pool)) @ w_pool_br[l]

        merged = jax.nn.sigmoid(g_attn) * y_attn + jax.nn.sigmoid(g_pool) * y_pool
        x = x + gate[:, None] * (merged @ w_out[l])
    return rmsnorm(x, final_g)
```

```python
import functools
import math

import jax
import jax.numpy as jnp
import numpy as np
from jax import lax
from jax.experimental import pallas as pl
from jax.experimental.pallas import tpu as pltpu

HEAD_DIM = 64
HEADS_PER_GROUP = 8
ATTN_GROUPS = ((128, 1), (512, 4), (2048, 16))
N_GROUPS = len(ATTN_GROUPS)
ATTN_WIDTH = HEADS_PER_GROUP * HEAD_DIM
QKV_GROUP_WIDTH = 3 * ATTN_WIDTH
QKV_WIDTH = N_GROUPS * QKV_GROUP_WIDTH
QBLOCK = 128
POOL_WINDOWS = (2, 4, 8, 16)
POOL_GROUPS = len(POOL_WINDOWS)
POOL_HALO = 16
NUM_BUCKETS = 32
MAX_DISTANCE = 2048
EPS = 1e-6
MASKED = -1e30
LANES = 128
VMEM_LIMIT_BYTES = 56 * 1024 * 1024


def _t5_bucket(n):
    max_exact = NUM_BUCKETS // 2
    nf = jnp.maximum(n, 1).astype(jnp.float32)
    large = max_exact + (jnp.log(nf / max_exact) / math.log(MAX_DISTANCE / max_exact)
                         * (NUM_BUCKETS - max_exact)).astype(jnp.int32)
    large = jnp.minimum(large, NUM_BUCKETS - 1)
    return jnp.where(n < max_exact, n, large)


def _bias_tables(bias_g, dil, n_back):
    i = jnp.arange(QBLOCK)[:, None]
    j = jnp.arange(2 * QBLOCK)[None, :]
    dist = QBLOCK + i - j
    ok = (dist >= 0) & (dist <= n_back)
    bucket = _t5_bucket(jnp.clip(dist, 0, n_back) * dil)
    bias = jnp.transpose(bias_g[bucket].astype(jnp.float32), (2, 0, 1))
    full = jnp.where(ok[None], bias, MASKED)
    first = jnp.where((ok & (j >= QBLOCK))[None], bias, MASKED)
    return jnp.stack([full, first], axis=0)


def _mod_kernel(c_ref, w_ref, b_ref, o_ref):
    o_ref[...] = jnp.dot(c_ref[...], w_ref[...], preferred_element_type=jnp.float32,
                         precision=lax.Precision.HIGHEST) + b_ref[...]


def _modulation(c, w_ada, b_ada):
    B, D = c.shape
    N = w_ada.shape[1]
    tn = 1024
    return pl.pallas_call(
        _mod_kernel,
        grid=(N // tn,),
        in_specs=[pl.BlockSpec((B, D), lambda j: (0, 0)),
                  pl.BlockSpec((D, tn), lambda j: (0, j)),
                  pl.BlockSpec((1, tn), lambda j: (0, j))],
        out_specs=pl.BlockSpec((B, tn), lambda j: (0, j)),
        out_shape=jax.ShapeDtypeStruct((B, N), jnp.float32),
        name="adaln_mod",
    )(c, w_ada, b_ada.reshape(1, N))


def _prenorm_kernel(x_ref, g_ref, mod_ref, h_ref):
    x = x_ref[0]
    y = x * lax.rsqrt(jnp.mean(x * x, axis=-1, keepdims=True) + EPS)
    y = y * g_ref[...]
    shift = mod_ref[0, 0:1, :]
    scale = mod_ref[0, 1:2, :]
    h_ref[0] = (y * (1.0 + scale) + shift).astype(h_ref.dtype)


def _prenorm(x, norm_g, mod3):
    B, S, D = x.shape
    ts = 1024
    return pl.pallas_call(
        _prenorm_kernel,
        grid=(B, S // ts),
        in_specs=[pl.BlockSpec((1, ts, D), lambda b, i: (b, i, 0)),
                  pl.BlockSpec((1, D), lambda b, i: (0, 0)),
                  pl.BlockSpec((1, 3, D), lambda b, i: (b, 0, 0))],
        out_specs=pl.BlockSpec((1, ts, D), lambda b, i: (b, i, 0)),
        out_shape=jax.ShapeDtypeStruct((B, S, D), jnp.bfloat16),
        name="prenorm",
    )(x, norm_g.reshape(1, D), mod3)


def _qkv_kernel(h_ref, w_ref, o_ref):
    o_ref[0, 0] = jnp.dot(h_ref[0], w_ref[...],
                          preferred_element_type=jnp.float32).astype(o_ref.dtype)


def _qkv_proj(h, w_qkv, dil):
    B, S, D = h.shape
    L = S // dil
    tm = min(1024, L)
    h_sub = h.reshape(B, L, dil * D)
    N = w_qkv.shape[1]
    return pl.pallas_call(
        _qkv_kernel,
        grid=(B, dil, L // tm),
        in_specs=[pl.BlockSpec((1, tm, D), lambda b, r, i: (b, i, r)),
                  pl.BlockSpec((D, N), lambda b, r, i: (0, 0))],
        out_specs=pl.BlockSpec((1, 1, tm, N), lambda b, r, i: (b, r, i, 0)),
        out_shape=jax.ShapeDtypeStruct((B, dil, L, N), jnp.bfloat16),
        compiler_params=pltpu.CompilerParams(vmem_limit_bytes=VMEM_LIMIT_BYTES),
        name=f"qkv_proj_d{dil}",
    )(h_sub, w_qkv)


def _attn_kernel(q_ref, kc_ref, kp_ref, vc_ref, vp_ref, bias_ref, o_ref, lse_ref, *, rows):
    n = pl.program_id(2)
    first_sel = jnp.where(n == 0, 1, 0)
    lane = lax.broadcasted_iota(jnp.int32, (1, LANES), 1)
    head_lanes = [lane < HEAD_DIM, lane >= HEAD_DIM]

    for j in range(rows // QBLOCK):
        qs = slice(j * QBLOCK, (j + 1) * QBLOCK)
        for hp in range(HEADS_PER_GROUP // 2):
            cs = slice(hp * LANES, (hp + 1) * LANES)
            q_pair = q_ref[0, 0, qs, cs]
            if j == 0:
                k_pair = jnp.concatenate([kp_ref[0, 0, :, cs], kc_ref[0, 0, 0:QBLOCK, cs]], axis=0)
                v_pair = jnp.concatenate([vp_ref[0, 0, :, cs], vc_ref[0, 0, 0:QBLOCK, cs]], axis=0)
            else:
                ks = slice((j - 1) * QBLOCK, (j + 1) * QBLOCK)
                k_pair = kc_ref[0, 0, ks, cs]
                v_pair = vc_ref[0, 0, ks, cs]
            acc = None
            m_e, l_e = [], []
            for e in range(2):
                h = 2 * hp + e
                q_h = jnp.where(head_lanes[e], q_pair, jnp.zeros_like(q_pair))
                s = lax.dot_general(q_h, k_pair, (((1,), (1,)), ((), ())),
                                    preferred_element_type=jnp.float32)
                if j == 0:
                    s = s + bias_ref[first_sel, h]
                else:
                    s = s + bias_ref[0, h]
                m = jnp.max(s, axis=-1, keepdims=True)
                p = jnp.exp(s - m)
                l = jnp.sum(p, axis=-1, keepdims=True)
                v_h = jnp.where(head_lanes[e], v_pair, jnp.zeros_like(v_pair))
                pv = jnp.dot(p.astype(v_h.dtype), v_h, preferred_element_type=jnp.float32)
                acc = pv if acc is None else acc + pv
                m_e.append(m)
                l_e.append(l)
            inv = jnp.where(head_lanes[0], 1.0 / l_e[0], 1.0 / l_e[1])
            lse = jnp.where(head_lanes[0], m_e[0] + jnp.log(l_e[0]), m_e[1] + jnp.log(l_e[1]))
            o_ref[0, qs, cs] = (acc * inv).astype(o_ref.dtype)
            lse_ref[0, qs, cs] = lse


def _attention(qkv, bias_tab, dil):
    B, _, L, _ = qkv.shape
    rows = min(512, L)
    rpb = rows // QBLOCK
    W = ATTN_WIDTH
    kernel = functools.partial(_attn_kernel, rows=rows)
    cur = lambda col: pl.BlockSpec((1, 1, rows, W), lambda b, r, n: (b, r, n, col))
    prev = lambda col: pl.BlockSpec((1, 1, QBLOCK, W),
                                    lambda b, r, n: (b, r, jnp.maximum(n * rpb - 1, 0), col))
    out_spec = pl.BlockSpec((1, rows, W), lambda b, r, n: (b, n, r))
    o, lse = pl.pallas_call(
        kernel,
        grid=(B, dil, L // rows),
        in_specs=[cur(0), cur(1), prev(1), cur(2), prev(2),
                  pl.BlockSpec((2, HEADS_PER_GROUP, QBLOCK, 2 * QBLOCK), lambda b, r, n: (0, 0, 0, 0))],
        out_specs=[out_spec, out_spec],
        out_shape=[jax.ShapeDtypeStruct((B, L, dil * W), jnp.bfloat16),
                   jax.ShapeDtypeStruct((B, L, dil * W), jnp.float32)],
        compiler_params=pltpu.CompilerParams(vmem_limit_bytes=VMEM_LIMIT_BYTES),
        name=f"band_attn_d{dil}",
    )(qkv, qkv, qkv, qkv, qkv, bias_tab)
    S = L * dil
    return o.reshape(B, S, W), lse.reshape(B, S, W)


def _sigmoid(v):
    return 1.0 / (1.0 + jnp.exp(-v))


def _out_kernel(h_ref, hprev_ref, x_ref, mod_ref,
                o0_ref, o1_ref, o2_ref, l0_ref, l1_ref, l2_ref,
                w_rest_ref, pool_w_ref, pool_scale_ref, w_attn_ref, w_pool_ref, w_out_ref, fin_ref,
                out_ref, ext_ref, *, tm, apply_final_norm):
    i = pl.program_id(1)
    W = ATTN_WIDTH
    h = h_ref[0]
    rest = jnp.dot(h, w_rest_ref[...], preferred_element_type=jnp.float32)
    z_attn = rest[:, 0:W]
    u_pool = rest[:, W:2 * W]
    z_pool = rest[:, 2 * W:3 * W]
    g_attn = rest[:, 3 * W:3 * W + 2 * W]
    g_pool = rest[:, 5 * W:7 * W]

    l0, l1, l2 = l0_ref[0], l1_ref[0], l2_ref[0]
    mx = jnp.maximum(jnp.maximum(l0, l1), l2)
    e0, e1, e2 = jnp.exp(l0 - mx), jnp.exp(l1 - mx), jnp.exp(l2 - mx)
    num = (e0 * o0_ref[0].astype(jnp.float32) + e1 * o1_ref[0].astype(jnp.float32)
           + e2 * o2_ref[0].astype(jnp.float32))
    attn = num / (e0 + e1 + e2)
    y_attn = jnp.dot((attn * (z_attn * _sigmoid(z_attn))).astype(jnp.bfloat16), w_attn_ref[...],
                     preferred_element_type=jnp.float32)

    u_prev = jnp.dot(hprev_ref[0], w_rest_ref[:, W:2 * W], preferred_element_type=jnp.float32)
    u_prev = jnp.where(i == 0, 0.0, u_prev)
    ext_ref[0:POOL_HALO, :] = u_prev
    ext_ref[POOL_HALO:POOL_HALO + tm, :] = u_pool
    t = i * tm + lax.broadcasted_iota(jnp.int32, (tm, 1), 0)
    mixed = []
    for g, win in enumerate(POOL_WINDOWS):
        cs = slice(g * LANES, (g + 1) * LANES)
        s = ext_ref[POOL_HALO:POOL_HALO + tm, cs]
        for back in range(1, win):
            s = s + ext_ref[POOL_HALO - back:POOL_HALO - back + tm, cs]
        cnt = jnp.minimum(t + 1, win).astype(jnp.float32)
        pooled = s / cnt - u_pool[:, cs]
        mixed.append(jnp.dot(pooled.astype(jnp.bfloat16), pool_w_ref[g],
                             preferred_element_type=jnp.float32))
    mixed = jnp.concatenate(mixed, axis=-1) * pool_scale_ref[...]
    y_pool = jnp.dot((mixed * (z_pool * _sigmoid(z_pool))).astype(jnp.bfloat16), w_pool_ref[...],
                     preferred_element_type=jnp.float32)

    merged = _sigmoid(g_attn) * y_attn + _sigmoid(g_pool) * y_pool
    gate = mod_ref[0, 2:3, :]
    y = x_ref[0] + gate * jnp.dot(merged.astype(jnp.bfloat16), w_out_ref[...],
                                  preferred_element_type=jnp.float32)
    if apply_final_norm:
        y = y * lax.rsqrt(jnp.mean(y * y, axis=-1, keepdims=True) + EPS) * fin_ref[...]
    out_ref[0] = y


def _merge_out(h, x, mod3, attn_parts, w_rest, pool_w, pool_scale, w_attn_br, w_pool_br, w_out,
               final_g, apply_final_norm):
    B, S, D = x.shape
    tm = 256
    W = ATTN_WIDTH
    hb = tm // POOL_HALO
    tok = lambda width: pl.BlockSpec((1, tm, width), lambda b, i: (b, i, 0))
    const = lambda shape: pl.BlockSpec(shape, lambda b, i: (0,) * len(shape))
    (o0, l0), (o1, l1), (o2, l2) = attn_parts
    kernel = functools.partial(_out_kernel, tm=tm, apply_final_norm=apply_final_norm)
    return pl.pallas_call(
        kernel,
        grid=(B, S // tm),
        in_specs=[tok(D),
                  pl.BlockSpec((1, POOL_HALO, D), lambda b, i: (b, jnp.maximum(i * hb - 1, 0), 0)),
                  tok(D),
                  pl.BlockSpec((1, 3, D), lambda b, i: (b, 0, 0)),
                  tok(W), tok(W), tok(W), tok(W), tok(W), tok(W),
                  const(w_rest.shape), const(pool_w.shape), const((1, POOL_GROUPS * LANES)),
                  const(w_attn_br.shape), const(w_pool_br.shape), const(w_out.shape), const((1, D))],
        out_specs=tok(D),
        out_shape=jax.ShapeDtypeStruct((B, S, D), x.dtype),
        scratch_shapes=[pltpu.VMEM((tm + POOL_HALO, POOL_GROUPS * LANES), jnp.float32)],
        compiler_params=pltpu.CompilerParams(vmem_limit_bytes=VMEM_LIMIT_BYTES),
        name="merge_out",
    )(h, h, x, mod3, o0, o1, o2, l0, l1, l2,
      w_rest, pool_w, pool_scale.reshape(1, -1), w_attn_br, w_pool_br, w_out, final_g.reshape(1, D))


def kernel(x, c, norm_g, w_ada, b_ada, w_in, pool_w, pool_scale, w_attn_br, w_pool_br, w_out, rel_bias, final_g):
    B, S, D = x.shape
    depth = norm_g.shape[0]
    bf16 = jnp.bfloat16
    col = np.arange(QKV_WIDTH) % QKV_GROUP_WIDTH
    q_scale = jnp.asarray(np.where(col < ATTN_WIDTH, HEAD_DIM ** -0.5, 1.0), jnp.float32)
    tables = []
    for gi, (win, dil) in enumerate(ATTN_GROUPS):
        assert win // dil <= QBLOCK and S % (dil * QBLOCK) == 0
        tables.append(_bias_tables(rel_bias[:, gi * HEADS_PER_GROUP:(gi + 1) * HEADS_PER_GROUP], dil, win // dil))

    for l in range(depth):
        w_qkv = (w_in[l, :, :QKV_WIDTH] * q_scale).astype(bf16)
        w_rest = w_in[l, :, QKV_WIDTH:].astype(bf16)
        mod3 = _modulation(c, w_ada[l], b_ada[l]).reshape(B, 3, D)
        h = _prenorm(x, norm_g[l], mod3)
        attn_parts = []
        for gi, (win, dil) in enumerate(ATTN_GROUPS):
            qkv = _qkv_proj(h, w_qkv[:, gi * QKV_GROUP_WIDTH:(gi + 1) * QKV_GROUP_WIDTH], dil)
            attn_parts.append(_attention(qkv, tables[gi], dil))
        x = _merge_out(h, x, mod3, attn_parts, w_rest, pool_w[l].astype(bf16), pool_scale[l],
                       w_attn_br[l].astype(bf16), w_pool_br[l].astype(bf16), w_out[l].astype(bf16),
                       final_g, apply_final_norm=(l == depth - 1))
    return x
```

```python
import functools
import math

import jax
import jax.numpy as jnp
import numpy as np
from jax import lax
from jax.experimental import pallas as pl
from jax.experimental.pallas import tpu as pltpu

HEAD_DIM = 64
HEADS_PER_GROUP = 8
ATTN_GROUPS = ((128, 1), (512, 4), (2048, 16))
N_GROUPS = len(ATTN_GROUPS)
ATTN_WIDTH = HEADS_PER_GROUP * HEAD_DIM
QKV_GROUP_WIDTH = 3 * ATTN_WIDTH
QKV_WIDTH = N_GROUPS * QKV_GROUP_WIDTH
QBLOCK = 128
POOL_WINDOWS = (2, 4, 8, 16)
POOL_GROUPS = len(POOL_WINDOWS)
POOL_HALO = 16
NUM_BUCKETS = 32
MAX_DISTANCE = 2048
EPS = 1e-6
MASKED = -1e30
LANES = 128
VMEM_LIMIT_BYTES = 56 * 1024 * 1024


def _t5_bucket(n):
    max_exact = NUM_BUCKETS // 2
    nf = jnp.maximum(n, 1).astype(jnp.float32)
    large = max_exact + (jnp.log(nf / max_exact) / math.log(MAX_DISTANCE / max_exact)
                         * (NUM_BUCKETS - max_exact)).astype(jnp.int32)
    large = jnp.minimum(large, NUM_BUCKETS - 1)
    return jnp.where(n < max_exact, n, large)


def _bias_tables(bias_g, dil, n_back):
    i = jnp.arange(QBLOCK)[:, None]
    j = jnp.arange(2 * QBLOCK)[None, :]
    dist = QBLOCK + i - j
    ok = (dist >= 0) & (dist <= n_back)
    bucket = _t5_bucket(jnp.clip(dist, 0, n_back) * dil)
    onehot = (bucket[None] == jnp.arange(NUM_BUCKETS)[:, None, None]).astype(jnp.float32)
    bias = jnp.einsum('bh,bqk->hqk', bias_g.astype(jnp.float32), onehot,
                      precision=lax.Precision.HIGHEST)
    full = jnp.where(ok[None], bias, MASKED)
    first = jnp.where((ok & (j >= QBLOCK))[None], bias, MASKED)
    return jnp.stack([full, first], axis=0)


def _mod_kernel(c_ref, w_ref, b_ref, o_ref):
    o_ref[...] = jnp.dot(c_ref[...], w_ref[...], preferred_element_type=jnp.float32,
                         precision=lax.Precision.HIGHEST) + b_ref[...]


def _modulation(c, w_ada, b_ada):
    B, D = c.shape
    N = w_ada.shape[1]
    tn = 1024
    return pl.pallas_call(
        _mod_kernel,
        grid=(N // tn,),
        in_specs=[pl.BlockSpec((B, D), lambda j: (0, 0)),
                  pl.BlockSpec((D, tn), lambda j: (0, j)),
                  pl.BlockSpec((1, tn), lambda j: (0, j))],
        out_specs=pl.BlockSpec((B, tn), lambda j: (0, j)),
        out_shape=jax.ShapeDtypeStruct((B, N), jnp.float32),
        name="adaln_mod",
    )(c, w_ada, b_ada.reshape(1, N))


def _prenorm_kernel(x_ref, g_ref, mod_ref, h_ref, *rest, dils, ts):
    sub_refs, stage_ref = rest[:-1], rest[-1]
    x = x_ref[0]
    y = x * lax.rsqrt(jnp.mean(x * x, axis=-1, keepdims=True) + EPS)
    y = y * g_ref[...]
    shift = mod_ref[0, 0:1, :]
    scale = mod_ref[0, 1:2, :]
    h = y * (1.0 + scale) + shift
    h_ref[0] = h.astype(h_ref.dtype)
    n_slabs = h.shape[-1] // LANES
    for cb in range(n_slabs):
        stage_ref[cb] = h[:, cb * LANES:(cb + 1) * LANES]
    for d, sub_ref in zip(dils, sub_refs):
        for r in range(d):
            for cb in range(n_slabs):
                sub_ref[0, r, :, cb * LANES:(cb + 1) * LANES] = (
                    stage_ref[cb, pl.ds(r, ts // d, stride=d), :].astype(sub_ref.dtype))


def _prenorm(x, norm_g, mod3, dils):
    B, S, D = x.shape
    ts = 1024
    kernel = functools.partial(_prenorm_kernel, dils=dils, ts=ts)
    return pl.pallas_call(
        kernel,
        grid=(B, S // ts),
        in_specs=[pl.BlockSpec((1, ts, D), lambda b, i: (b, i, 0)),
                  pl.BlockSpec((1, D), lambda b, i: (0, 0)),
                  pl.BlockSpec((1, 3, D), lambda b, i: (b, 0, 0))],
        out_specs=[pl.BlockSpec((1, ts, D), lambda b, i: (b, i, 0))]
                  + [pl.BlockSpec((1, d, ts // d, D), lambda b, i: (b, 0, i, 0)) for d in dils],
        out_shape=[jax.ShapeDtypeStruct((B, S, D), jnp.bfloat16)]
                  + [jax.ShapeDtypeStruct((B, d, S // d, D), jnp.bfloat16) for d in dils],
        scratch_shapes=[pltpu.VMEM((D // LANES, ts, LANES), jnp.float32)],
        compiler_params=pltpu.CompilerParams(vmem_limit_bytes=VMEM_LIMIT_BYTES),
        name="prenorm",
    )(x, norm_g.reshape(1, D), mod3)


def _qkv_kernel(h_ref, w_ref, o_ref):
    o_ref[0, 0] = jnp.dot(h_ref[0, 0], w_ref[...],
                          preferred_element_type=jnp.float32).astype(o_ref.dtype)


def _qkv_proj(h_sub, w_qkv):
    B, dil, L, D = h_sub.shape
    tm = min(1024, L)
    N = w_qkv.shape[1]
    return pl.pallas_call(
        _qkv_kernel,
        grid=(B, dil, L // tm),
        in_specs=[pl.BlockSpec((1, 1, tm, D), lambda b, r, i: (b, r, i, 0)),
                  pl.BlockSpec((D, N), lambda b, r, i: (0, 0))],
        out_specs=pl.BlockSpec((1, 1, tm, N), lambda b, r, i: (b, r, i, 0)),
        out_shape=jax.ShapeDtypeStruct((B, dil, L, N), jnp.bfloat16),
        compiler_params=pltpu.CompilerParams(vmem_limit_bytes=VMEM_LIMIT_BYTES),
        name=f"qkv_proj_d{dil}",
    )(h_sub, w_qkv)


def _attn_kernel(q_ref, kc_ref, kp_ref, vc_ref, vp_ref, bias_ref, o_ref, lse_ref, *, rows):
    n = pl.program_id(2)
    first_sel = jnp.where(n == 0, 1, 0)
    lane = lax.broadcasted_iota(jnp.int32, (1, LANES), 1)
    head_lanes = [lane < HEAD_DIM, lane >= HEAD_DIM]

    for j in range(rows // QBLOCK):
        qs = slice(j * QBLOCK, (j + 1) * QBLOCK)
        for hp in range(HEADS_PER_GROUP // 2):
            cs = slice(hp * LANES, (hp + 1) * LANES)
            q_pair = q_ref[0, 0, qs, cs]
            if j == 0:
                k_pair = jnp.concatenate([kp_ref[0, 0, :, cs], kc_ref[0, 0, 0:QBLOCK, cs]], axis=0)
                v_pair = jnp.concatenate([vp_ref[0, 0, :, cs], vc_ref[0, 0, 0:QBLOCK, cs]], axis=0)
            else:
                ks = slice((j - 1) * QBLOCK, (j + 1) * QBLOCK)
                k_pair = kc_ref[0, 0, ks, cs]
                v_pair = vc_ref[0, 0, ks, cs]
            acc = None
            m_e, l_e = [], []
            for e in range(2):
                h = 2 * hp + e
                q_h = jnp.where(head_lanes[e], q_pair, jnp.zeros_like(q_pair))
                s = lax.dot_general(q_h, k_pair, (((1,), (1,)), ((), ())),
                                    preferred_element_type=jnp.float32)
                if j == 0:
                    s = s + bias_ref[first_sel, h]
                else:
                    s = s + bias_ref[0, h]
                m = jnp.max(s, axis=-1, keepdims=True)
                p = jnp.exp(s - m)
                l = jnp.sum(p, axis=-1, keepdims=True)
                v_h = jnp.where(head_lanes[e], v_pair, jnp.zeros_like(v_pair))
                pv = jnp.dot(p.astype(v_h.dtype), v_h, preferred_element_type=jnp.float32)
                acc = pv if acc is None else acc + pv
                m_e.append(m)
                l_e.append(l)
            inv = jnp.where(head_lanes[0], 1.0 / l_e[0], 1.0 / l_e[1])
            lse = jnp.where(head_lanes[0], m_e[0] + jnp.log(l_e[0]), m_e[1] + jnp.log(l_e[1]))
            o_ref[0, 0, qs, cs] = (acc * inv).astype(o_ref.dtype)
            lse_ref[0, 0, qs, cs] = lse


def _attention(qkv, bias_tab):
    B, dil, L, _ = qkv.shape
    rows = min(512, L)
    rpb = rows // QBLOCK
    W = ATTN_WIDTH
    kernel = functools.partial(_attn_kernel, rows=rows)
    cur = lambda col: pl.BlockSpec((1, 1, rows, W), lambda b, r, n: (b, r, n, col))
    prev = lambda col: pl.BlockSpec((1, 1, QBLOCK, W),
                                    lambda b, r, n: (b, r, jnp.maximum(n * rpb - 1, 0), col))
    out_spec = pl.BlockSpec((1, 1, rows, W), lambda b, r, n: (b, r, n, 0))
    return pl.pallas_call(
        kernel,
        grid=(B, dil, L // rows),
        in_specs=[cur(0), cur(1), prev(1), cur(2), prev(2),
                  pl.BlockSpec((2, HEADS_PER_GROUP, QBLOCK, 2 * QBLOCK), lambda b, r, n: (0, 0, 0, 0))],
        out_specs=[out_spec, out_spec],
        out_shape=[jax.ShapeDtypeStruct((B, dil, L, W), jnp.bfloat16),
                   jax.ShapeDtypeStruct((B, dil, L, W), jnp.float32)],
        compiler_params=pltpu.CompilerParams(vmem_limit_bytes=VMEM_LIMIT_BYTES),
        name=f"band_attn_d{dil}",
    )(qkv, qkv, qkv, qkv, qkv, bias_tab)


def _sigmoid(v):
    return 1.0 / (1.0 + jnp.exp(-v))


def _token_order(ref, stage_ref, tm):
    d = ref.shape[1]
    if d == 1:
        return ref[0, 0].astype(jnp.float32)
    n_slabs = ref.shape[-1] // LANES
    for r in range(d):
        for cb in range(n_slabs):
            stage_ref[cb, pl.ds(r, tm // d, stride=d), :] = (
                ref[0, r, :, cb * LANES:(cb + 1) * LANES].astype(jnp.float32))
    return jnp.concatenate([stage_ref[cb] for cb in range(n_slabs)], axis=-1)


def _out_kernel(h_ref, hprev_ref, x_ref, mod_ref,
                o0_ref, o1_ref, o2_ref, l0_ref, l1_ref, l2_ref,
                w_rest_ref, pool_w_ref, pool_scale_ref, w_attn_ref, w_pool_ref, w_out_ref, fin_ref,
                out_ref, ext_ref, so1_ref, so2_ref, sl1_ref, sl2_ref, *, tm, apply_final_norm):
    i = pl.program_id(1)
    W = ATTN_WIDTH
    h = h_ref[0]
    rest = jnp.dot(h, w_rest_ref[...], preferred_element_type=jnp.float32)
    z_attn = rest[:, 0:W]
    u_pool = rest[:, W:2 * W]
    z_pool = rest[:, 2 * W:3 * W]
    g_attn = rest[:, 3 * W:3 * W + 2 * W]
    g_pool = rest[:, 5 * W:7 * W]

    l0 = _token_order(l0_ref, None, tm)
    l1 = _token_order(l1_ref, sl1_ref, tm)
    l2 = _token_order(l2_ref, sl2_ref, tm)
    mx = jnp.maximum(jnp.maximum(l0, l1), l2)
    e0, e1, e2 = jnp.exp(l0 - mx), jnp.exp(l1 - mx), jnp.exp(l2 - mx)
    num = (e0 * _token_order(o0_ref, None, tm) + e1 * _token_order(o1_ref, so1_ref, tm)
           + e2 * _token_order(o2_ref, so2_ref, tm))
    attn = num / (e0 + e1 + e2)
    y_attn = jnp.dot((attn * (z_attn * _sigmoid(z_attn))).astype(jnp.bfloat16), w_attn_ref[...],
                     preferred_element_type=jnp.float32)

    u_prev = jnp.dot(hprev_ref[0], w_rest_ref[:, W:2 * W], preferred_element_type=jnp.float32)
    u_prev = jnp.where(i == 0, 0.0, u_prev)
    ext_ref[0:POOL_HALO, :] = u_prev
    ext_ref[POOL_HALO:POOL_HALO + tm, :] = u_pool
    t = i * tm + lax.broadcasted_iota(jnp.int32, (tm, 1), 0)
    mixed = []
    for g, win in enumerate(POOL_WINDOWS):
        cs = slice(g * LANES, (g + 1) * LANES)
        s = ext_ref[POOL_HALO:POOL_HALO + tm, cs]
        for back in range(1, win):
            s = s + ext_ref[POOL_HALO - back:POOL_HALO - back + tm, cs]
        cnt = jnp.minimum(t + 1, win).astype(jnp.float32)
        pooled = s / cnt - u_pool[:, cs]
        mixed.append(jnp.dot(pooled.astype(jnp.bfloat16), pool_w_ref[g],
                             preferred_element_type=jnp.float32))
    mixed = jnp.concatenate(mixed, axis=-1) * pool_scale_ref[...]
    y_pool = jnp.dot((mixed * (z_pool * _sigmoid(z_pool))).astype(jnp.bfloat16), w_pool_ref[...],
                     preferred_element_type=jnp.float32)

    merged = _sigmoid(g_attn) * y_attn + _sigmoid(g_pool) * y_pool
    gate = mod_ref[0, 2:3, :]
    y = x_ref[0] + gate * jnp.dot(merged.astype(jnp.bfloat16), w_out_ref[...],
                                  preferred_element_type=jnp.float32)
    if apply_final_norm:
        y = y * lax.rsqrt(jnp.mean(y * y, axis=-1, keepdims=True) + EPS) * fin_ref[...]
    out_ref[0] = y


def _merge_out(h, x, mod3, attn_parts, w_rest, pool_w, pool_scale, w_attn_br, w_pool_br, w_out,
               final_g, apply_final_norm):
    B, S, D = x.shape
    tm = 256
    W = ATTN_WIDTH
    hb = tm // POOL_HALO
    tok = lambda width: pl.BlockSpec((1, tm, width), lambda b, i: (b, i, 0))
    const = lambda shape: pl.BlockSpec(shape, lambda b, i: (0,) * len(shape))
    (o0, l0), (o1, l1), (o2, l2) = attn_parts
    sub = lambda a: pl.BlockSpec((1, a.shape[1], tm // a.shape[1], W), lambda b, i: (b, 0, i, 0))
    stage = pltpu.VMEM((W // LANES, tm, LANES), jnp.float32)
    kernel = functools.partial(_out_kernel, tm=tm, apply_final_norm=apply_final_norm)
    return pl.pallas_call(
        kernel,
        grid=(B, S // tm),
        in_specs=[tok(D),
                  pl.BlockSpec((1, POOL_HALO, D), lambda b, i: (b, jnp.maximum(i * hb - 1, 0), 0)),
                  tok(D),
                  pl.BlockSpec((1, 3, D), lambda b, i: (b, 0, 0)),
                  sub(o0), sub(o1), sub(o2), sub(l0), sub(l1), sub(l2),
                  const(w_rest.shape), const(pool_w.shape), const((1, POOL_GROUPS * LANES)),
                  const(w_attn_br.shape), const(w_pool_br.shape), const(w_out.shape), const((1, D))],
        out_specs=tok(D),
        out_shape=jax.ShapeDtypeStruct((B, S, D), x.dtype),
        scratch_shapes=[pltpu.VMEM((tm + POOL_HALO, POOL_GROUPS * LANES), jnp.float32),
                        stage, stage, stage, stage],
        compiler_params=pltpu.CompilerParams(vmem_limit_bytes=VMEM_LIMIT_BYTES),
        name="merge_out",
    )(h, h, x, mod3, o0, o1, o2, l0, l1, l2,
      w_rest, pool_w, pool_scale.reshape(1, -1), w_attn_br, w_pool_br, w_out, final_g.reshape(1, D))


def kernel(x, c, norm_g, w_ada, b_ada, w_in, pool_w, pool_scale, w_attn_br, w_pool_br, w_out, rel_bias, final_g):
    B, S, D = x.shape
    depth = norm_g.shape[0]
    bf16 = jnp.bfloat16
    col = np.arange(QKV_WIDTH) % QKV_GROUP_WIDTH
    q_scale = jnp.asarray(np.where(col < ATTN_WIDTH, HEAD_DIM ** -0.5, 1.0), jnp.float32)
    tables = []
    for gi, (win, dil) in enumerate(ATTN_GROUPS):
        assert win // dil <= QBLOCK and S % (dil * QBLOCK) == 0
        tables.append(_bias_tables(rel_bias[:, gi * HEADS_PER_GROUP:(gi + 1) * HEADS_PER_GROUP], dil, win // dil))

    for l in range(depth):
        w_qkv = (w_in[l, :, :QKV_WIDTH] * q_scale).astype(bf16)
        w_rest = w_in[l, :, QKV_WIDTH:].astype(bf16)
        mod3 = _modulation(c, w_ada[l], b_ada[l]).reshape(B, 3, D)
        dils = tuple(dil for _, dil in ATTN_GROUPS if dil > 1)
        h, *h_subs = _prenorm(x, norm_g[l], mod3, dils)
        h_by_dil = dict(zip(dils, h_subs))
        h_by_dil[1] = h.reshape(B, 1, S, D)
        attn_parts = []
        for gi, (win, dil) in enumerate(ATTN_GROUPS):
            qkv = _qkv_proj(h_by_dil[dil], w_qkv[:, gi * QKV_GROUP_WIDTH:(gi + 1) * QKV_GROUP_WIDTH])
            attn_parts.append(_attention(qkv, tables[gi]))
        x = _merge_out(h, x, mod3, attn_parts, w_rest, pool_w[l].astype(bf16), pool_scale[l],
                       w_attn_br[l].astype(bf16), w_pool_br[l].astype(bf16), w_out[l].astype(bf16),
                       final_g, apply_final_norm=(l == depth - 1))
    return x
```

```python
import functools
import math

import jax
import jax.numpy as jnp
import numpy as np
from jax import lax
from jax.experimental import pallas as pl
from jax.experimental.pallas import tpu as pltpu

HEAD_DIM = 64
HEADS_PER_GROUP = 8
ATTN_GROUPS = ((128, 1), (512, 4), (2048, 16))
N_GROUPS = len(ATTN_GROUPS)
ATTN_WIDTH = HEADS_PER_GROUP * HEAD_DIM
QKV_GROUP_WIDTH = 3 * ATTN_WIDTH
QKV_WIDTH = N_GROUPS * QKV_GROUP_WIDTH
QBLOCK = 128
ATTN_STEP_ROWS = 1024
ATTN_PIPELINE_LAG = 4
POOL_WINDOWS = (2, 4, 8, 16)
POOL_GROUPS = len(POOL_WINDOWS)
POOL_HALO = 16
NUM_BUCKETS = 32
MAX_DISTANCE = 2048
EPS = 1e-6
MASKED = -1e30
LOG2E = math.log2(math.e)
LN2 = math.log(2.0)
LANES = 128
VMEM_LIMIT_BYTES = 56 * 1024 * 1024


def _t5_bucket(n):
    max_exact = NUM_BUCKETS // 2
    nf = jnp.maximum(n, 1).astype(jnp.float32)
    large = max_exact + (jnp.log(nf / max_exact) / math.log(MAX_DISTANCE / max_exact)
                         * (NUM_BUCKETS - max_exact)).astype(jnp.int32)
    large = jnp.minimum(large, NUM_BUCKETS - 1)
    return jnp.where(n < max_exact, n, large)


def _bias_tables(bias_g, dil, n_back):
    i = jnp.arange(QBLOCK)[:, None]
    j = jnp.arange(2 * QBLOCK)[None, :]
    dist = QBLOCK + i - j
    ok = (dist >= 0) & (dist <= n_back)
    bucket = _t5_bucket(jnp.clip(dist, 0, n_back) * dil)
    onehot = (bucket[None] == jnp.arange(NUM_BUCKETS)[:, None, None]).astype(jnp.float32)
    bias = jnp.einsum('bh,bqk->hqk', bias_g.astype(jnp.float32), onehot,
                      precision=lax.Precision.HIGHEST)
    bias = bias * LOG2E
    full = jnp.where(ok[None], bias, MASKED)
    first = jnp.where((ok & (j >= QBLOCK))[None], bias, MASKED)
    return jnp.stack([full, first], axis=0)


def _mod_kernel(c_ref, w_ref, b_ref, o_ref):
    o_ref[...] = jnp.dot(c_ref[...], w_ref[...], preferred_element_type=jnp.float32,
                         precision=lax.Precision.HIGHEST) + b_ref[...]


def _modulation(c, w_ada, b_ada):
    B, D = c.shape
    N = w_ada.shape[1]
    tn = 1024
    return pl.pallas_call(
        _mod_kernel,
        grid=(N // tn,),
        in_specs=[pl.BlockSpec((B, D), lambda j: (0, 0)),
                  pl.BlockSpec((D, tn), lambda j: (0, j)),
                  pl.BlockSpec((1, tn), lambda j: (0, j))],
        out_specs=pl.BlockSpec((B, tn), lambda j: (0, j)),
        out_shape=jax.ShapeDtypeStruct((B, N), jnp.float32),
        name="adaln_mod",
    )(c, w_ada, b_ada.reshape(1, N))


def _prenorm_kernel(x_ref, g_ref, mod_ref, h_ref, *rest, dils, ts):
    sub_refs, stage_ref = rest[:-1], rest[-1]
    x = x_ref[0]
    y = x * lax.rsqrt(jnp.mean(x * x, axis=-1, keepdims=True) + EPS)
    y = y * g_ref[...]
    shift = mod_ref[0, 0:1, :]
    scale = mod_ref[0, 1:2, :]
    h = y * (1.0 + scale) + shift
    h_ref[0] = h.astype(h_ref.dtype)
    n_slabs = h.shape[-1] // LANES
    for cb in range(n_slabs):
        stage_ref[cb] = h[:, cb * LANES:(cb + 1) * LANES]
    for d, sub_ref in zip(dils, sub_refs):
        for r in range(d):
            for cb in range(n_slabs):
                sub_ref[0, r, :, cb * LANES:(cb + 1) * LANES] = (
                    stage_ref[cb, pl.ds(r, ts // d, stride=d), :].astype(sub_ref.dtype))


def _prenorm(x, norm_g, mod3, dils):
    B, S, D = x.shape
    ts = 1024
    kernel = functools.partial(_prenorm_kernel, dils=dils, ts=ts)
    return pl.pallas_call(
        kernel,
        grid=(B, S // ts),
        in_specs=[pl.BlockSpec((1, ts, D), lambda b, i: (b, i, 0)),
                  pl.BlockSpec((1, D), lambda b, i: (0, 0)),
                  pl.BlockSpec((1, 3, D), lambda b, i: (b, 0, 0))],
        out_specs=[pl.BlockSpec((1, ts, D), lambda b, i: (b, i, 0))]
                  + [pl.BlockSpec((1, d, ts // d, D), lambda b, i: (b, 0, i, 0)) for d in dils],
        out_shape=[jax.ShapeDtypeStruct((B, S, D), jnp.bfloat16)]
                  + [jax.ShapeDtypeStruct((B, d, S // d, D), jnp.bfloat16) for d in dils],
        scratch_shapes=[pltpu.VMEM((D // LANES, ts, LANES), jnp.float32)],
        compiler_params=pltpu.CompilerParams(vmem_limit_bytes=VMEM_LIMIT_BYTES),
        name="prenorm",
    )(x, norm_g.reshape(1, D), mod3)


def _qkv_kernel(h_ref, w_ref, o_ref):
    o_ref[0, 0] = jnp.dot(h_ref[0, 0], w_ref[...],
                          preferred_element_type=jnp.float32).astype(o_ref.dtype)


def _qkv_proj(h_sub, w_qkv):
    B, dil, L, D = h_sub.shape
    tm = min(1024, L)
    N = w_qkv.shape[1]
    return pl.pallas_call(
        _qkv_kernel,
        grid=(B, dil, L // tm),
        in_specs=[pl.BlockSpec((1, 1, tm, D), lambda b, r, i: (b, r, i, 0)),
                  pl.BlockSpec((D, N), lambda b, r, i: (0, 0))],
        out_specs=pl.BlockSpec((1, 1, tm, N), lambda b, r, i: (b, r, i, 0)),
        out_shape=jax.ShapeDtypeStruct((B, dil, L, N), jnp.bfloat16),
        compiler_params=pltpu.CompilerParams(vmem_limit_bytes=VMEM_LIMIT_BYTES),
        name=f"qkv_proj_d{dil}",
    )(h_sub, w_qkv)


def _attn_kernel(q_ref, kc_ref, kp_ref, vc_ref, vp_ref, bias_ref, ones_ref, o_ref, lse_ref, s_ref, m_ref, *, rows):
    n = pl.program_id(2)
    first_sel = jnp.where(n == 0, 1, 0)
    lane = lax.broadcasted_iota(jnp.int32, (1, LANES), 1)
    head_lanes = [lane < HEAD_DIM, lane >= HEAD_DIM]
    n_heads = HEADS_PER_GROUP

    blocks_per_sub = rows // QBLOCK

    def pair_operands(ref_c, ref_p, rl, j, cs):
        if j == 0:
            return jnp.concatenate([ref_p[0, rl, :, cs], ref_c[0, rl, 0:QBLOCK, cs]], axis=0)
        return ref_c[0, rl, (j - 1) * QBLOCK:(j + 1) * QBLOCK, cs]

    def unit(u):
        blk, h = divmod(u, n_heads)
        rl, j = divmod(blk, blocks_per_sub)
        hp, e = divmod(h, 2)
        return rl, j, h, e, slice(hp * LANES, (hp + 1) * LANES)

    def logits_pass(u):
        rl, j, h, e, cs = unit(u)
        q_pair = q_ref[0, rl, j * QBLOCK:(j + 1) * QBLOCK, cs]
        k_pair = pair_operands(kc_ref, kp_ref, rl, j, cs)
        q_h = jnp.where(head_lanes[e], q_pair, jnp.zeros_like(q_pair))
        s = lax.dot_general(q_h, k_pair, (((1,), (1,)), ((), ())), preferred_element_type=jnp.float32)
        s = s + (bias_ref[first_sel, h] if j == 0 else bias_ref[0, h])
        s_ref[u % n_slots] = s
        m_ref[u % n_slots] = jnp.broadcast_to(jnp.max(s, axis=-1, keepdims=True), (QBLOCK, LANES))

    def values_pass(u0):
        rl, j, _, _, cs = unit(u0)
        v_pair = pair_operands(vc_ref, vp_ref, rl, j, cs)
        logits = [(s_ref[(u0 + e) % n_slots], m_ref[(u0 + e) % n_slots]) for e in range(2)]
        p_e, rhs_e = [], []
        for e, (s, m) in enumerate(logits):
            p_e += [jnp.exp2(s[:, :LANES] - m), jnp.exp2(s[:, LANES:] - m)]
            rhs_e.append(jnp.concatenate(
                [jnp.where(head_lanes[e], v_pair, jnp.zeros_like(v_pair)), ones_ref[e]], axis=-1))
        p = jnp.concatenate(p_e, axis=-1).astype(v_pair.dtype)
        acc = jnp.dot(p, jnp.concatenate(rhs_e, axis=0), preferred_element_type=jnp.float32)
        denom = acc[:, LANES:]
        qs = slice(j * QBLOCK, (j + 1) * QBLOCK)
        o_ref[0, rl, qs, cs] = (acc[:, :LANES] / denom).astype(o_ref.dtype)
        lse_ref[0, rl, qs, cs] = (jnp.where(head_lanes[0], logits[0][1], logits[1][1]) + jnp.log2(denom)) * LN2

    n_units = q_ref.shape[1] * blocks_per_sub * n_heads
    n_slots = s_ref.shape[0]
    lag = n_slots // 2
    for u in range(-lag, n_units, 2):
        if u + lag < n_units:
            logits_pass(u + lag)
            logits_pass(u + lag + 1)
        if u >= 0:
            values_pass(u)


def _attention(qkv, bias_tab):
    B, dil, L, _ = qkv.shape
    rows = min(ATTN_STEP_ROWS, L)
    nsub = min(dil, ATTN_STEP_ROWS // rows)
    rpb = rows // QBLOCK
    W = ATTN_WIDTH
    kernel = functools.partial(_attn_kernel, rows=rows)
    ones_cols = jnp.asarray(np.broadcast_to(
        (np.arange(LANES) // HEAD_DIM == np.arange(2)[:, None])[:, None, :], (2, 2 * QBLOCK, LANES)), qkv.dtype)
    cur = lambda col: pl.BlockSpec((1, nsub, rows, W), lambda b, r, n: (b, r, n, col))
    prev = lambda col: pl.BlockSpec((1, nsub, QBLOCK, W),
                                    lambda b, r, n: (b, r, jnp.maximum(n * rpb - 1, 0), col))
    out_spec = pl.BlockSpec((1, nsub, rows, W), lambda b, r, n: (b, r, n, 0))
    return pl.pallas_call(
        kernel,
        grid=(B, dil // nsub, L // rows),
        in_specs=[cur(0), cur(1), prev(1), cur(2), prev(2),
                  pl.BlockSpec((2, HEADS_PER_GROUP, QBLOCK, 2 * QBLOCK), lambda b, r, n: (0, 0, 0, 0)),
                  pl.BlockSpec((2, 2 * QBLOCK, LANES), lambda b, r, n: (0, 0, 0))],
        out_specs=[out_spec, out_spec],
        out_shape=[jax.ShapeDtypeStruct((B, dil, L, W), jnp.bfloat16),
                   jax.ShapeDtypeStruct((B, dil, L, W), jnp.float32)],
        scratch_shapes=[pltpu.VMEM((2 * ATTN_PIPELINE_LAG, QBLOCK, 2 * QBLOCK), jnp.float32),
                        pltpu.VMEM((2 * ATTN_PIPELINE_LAG, QBLOCK, LANES), jnp.float32)],
        compiler_params=pltpu.CompilerParams(vmem_limit_bytes=VMEM_LIMIT_BYTES),
        name=f"band_attn_d{dil}",
    )(qkv, qkv, qkv, qkv, qkv, bias_tab, ones_cols)


def _sigmoid(v):
    return 1.0 / (1.0 + jnp.exp(-v))


def _token_order(ref, stage_ref, tm):
    d = ref.shape[1]
    if d == 1:
        return ref[0, 0].astype(jnp.float32)
    n_slabs = ref.shape[-1] // LANES
    for r in range(d):
        for cb in range(n_slabs):
            stage_ref[cb, pl.ds(r, tm // d, stride=d), :] = (
                ref[0, r, :, cb * LANES:(cb + 1) * LANES].astype(jnp.float32))
    return jnp.concatenate([stage_ref[cb] for cb in range(n_slabs)], axis=-1)


def _out_kernel(h_ref, hprev_ref, x_ref, mod_ref,
                o0_ref, o1_ref, o2_ref, l0_ref, l1_ref, l2_ref,
                w_rest_ref, pool_w_ref, pool_scale_ref, w_attn_ref, w_pool_ref, w_out_ref, fin_ref,
                out_ref, ext_ref, so1_ref, so2_ref, sl1_ref, sl2_ref, *, tm, apply_final_norm):
    i = pl.program_id(1)
    W = ATTN_WIDTH
    h = h_ref[0]
    rest = jnp.dot(h, w_rest_ref[...], preferred_element_type=jnp.float32)
    z_attn = rest[:, 0:W]
    u_pool = rest[:, W:2 * W]
    z_pool = rest[:, 2 * W:3 * W]
    g_attn = rest[:, 3 * W:3 * W + 2 * W]
    g_pool = rest[:, 5 * W:7 * W]

    l0 = _token_order(l0_ref, None, tm)
    l1 = _token_order(l1_ref, sl1_ref, tm)
    l2 = _token_order(l2_ref, sl2_ref, tm)
    mx = jnp.maximum(jnp.maximum(l0, l1), l2)
    e0, e1, e2 = jnp.exp(l0 - mx), jnp.exp(l1 - mx), jnp.exp(l2 - mx)
    num = (e0 * _token_order(o0_ref, None, tm) + e1 * _token_order(o1_ref, so1_ref, tm)
           + e2 * _token_order(o2_ref, so2_ref, tm))
    attn = num / (e0 + e1 + e2)
    y_attn = jnp.dot((attn * (z_attn * _sigmoid(z_attn))).astype(jnp.bfloat16), w_attn_ref[...],
                     preferred_element_type=jnp.float32)

    u_prev = jnp.dot(hprev_ref[0], w_rest_ref[:, W:2 * W], preferred_element_type=jnp.float32)
    u_prev = jnp.where(i == 0, 0.0, u_prev)
    ext_ref[0:POOL_HALO, :] = u_prev
    ext_ref[POOL_HALO:POOL_HALO + tm, :] = u_pool
    t = i * tm + lax.broadcasted_iota(jnp.int32, (tm, 1), 0)
    mixed = []
    for g, win in enumerate(POOL_WINDOWS):
        cs = slice(g * LANES, (g + 1) * LANES)
        s = ext_ref[POOL_HALO:POOL_HALO + tm, cs]
        for back in range(1, win):
            s = s + ext_ref[POOL_HALO - back:POOL_HALO - back + tm, cs]
        cnt = jnp.minimum(t + 1, win).astype(jnp.float32)
        pooled = s / cnt - u_pool[:, cs]
        mixed.append(jnp.dot(pooled.astype(jnp.bfloat16), pool_w_ref[g],
                             preferred_element_type=jnp.float32))
    mixed = jnp.concatenate(mixed, axis=-1) * pool_scale_ref[...]
    y_pool = jnp.dot((mixed * (z_pool * _sigmoid(z_pool))).astype(jnp.bfloat16), w_pool_ref[...],
                     preferred_element_type=jnp.float32)

    merged = _sigmoid(g_attn) * y_attn + _sigmoid(g_pool) * y_pool
    gate = mod_ref[0, 2:3, :]
    y = x_ref[0] + gate * jnp.dot(merged.astype(jnp.bfloat16), w_out_ref[...],
                                  preferred_element_type=jnp.float32)
    if apply_final_norm:
        y = y * lax.rsqrt(jnp.mean(y * y, axis=-1, keepdims=True) + EPS) * fin_ref[...]
    out_ref[0] = y


def _merge_out(h, x, mod3, attn_parts, w_rest, pool_w, pool_scale, w_attn_br, w_pool_br, w_out,
               final_g, apply_final_norm):
    B, S, D = x.shape
    tm = 256
    W = ATTN_WIDTH
    hb = tm // POOL_HALO
    tok = lambda width: pl.BlockSpec((1, tm, width), lambda b, i: (b, i, 0))
    const = lambda shape: pl.BlockSpec(shape, lambda b, i: (0,) * len(shape))
    (o0, l0), (o1, l1), (o2, l2) = attn_parts
    sub = lambda a: pl.BlockSpec((1, a.shape[1], tm // a.shape[1], W), lambda b, i: (b, 0, i, 0))
    stage = pltpu.VMEM((W // LANES, tm, LANES), jnp.float32)
    kernel = functools.partial(_out_kernel, tm=tm, apply_final_norm=apply_final_norm)
    return pl.pallas_call(
        kernel,
        grid=(B, S // tm),
        in_specs=[tok(D),
                  pl.BlockSpec((1, POOL_HALO, D), lambda b, i: (b, jnp.maximum(i * hb - 1, 0), 0)),
                  tok(D),
                  pl.BlockSpec((1, 3, D), lambda b, i: (b, 0, 0)),
                  sub(o0), sub(o1), sub(o2), sub(l0), sub(l1), sub(l2),
                  const(w_rest.shape), const(pool_w.shape), const((1, POOL_GROUPS * LANES)),
                  const(w_attn_br.shape), const(w_pool_br.shape), const(w_out.shape), const((1, D))],
        out_specs=tok(D),
        out_shape=jax.ShapeDtypeStruct((B, S, D), x.dtype),
        scratch_shapes=[pltpu.VMEM((tm + POOL_HALO, POOL_GROUPS * LANES), jnp.float32),
                        stage, stage, stage, stage],
        compiler_params=pltpu.CompilerParams(vmem_limit_bytes=VMEM_LIMIT_BYTES),
        name="merge_out",
    )(h, h, x, mod3, o0, o1, o2, l0, l1, l2,
      w_rest, pool_w, pool_scale.reshape(1, -1), w_attn_br, w_pool_br, w_out, final_g.reshape(1, D))


def kernel(x, c, norm_g, w_ada, b_ada, w_in, pool_w, pool_scale, w_attn_br, w_pool_br, w_out, rel_bias, final_g):
    B, S, D = x.shape
    depth = norm_g.shape[0]
    bf16 = jnp.bfloat16
    col = np.arange(QKV_WIDTH) % QKV_GROUP_WIDTH
    q_scale = jnp.asarray(np.where(col < ATTN_WIDTH, HEAD_DIM ** -0.5 * LOG2E, 1.0), jnp.float32)
    tables = []
    for gi, (win, dil) in enumerate(ATTN_GROUPS):
        assert win // dil <= QBLOCK and S % (dil * QBLOCK) == 0
        tables.append(_bias_tables(rel_bias[:, gi * HEADS_PER_GROUP:(gi + 1) * HEADS_PER_GROUP], dil, win // dil))

    for l in range(depth):
        w_qkv = (w_in[l, :, :QKV_WIDTH] * q_scale).astype(bf16)
        w_rest = w_in[l, :, QKV_WIDTH:].astype(bf16)
        mod3 = _modulation(c, w_ada[l], b_ada[l]).reshape(B, 3, D)
        dils = tuple(dil for _, dil in ATTN_GROUPS if dil > 1)
        h, *h_subs = _prenorm(x, norm_g[l], mod3, dils)
        h_by_dil = dict(zip(dils, h_subs))
        h_by_dil[1] = h.reshape(B, 1, S, D)
        attn_parts = []
        for gi, (win, dil) in enumerate(ATTN_GROUPS):
            qkv = _qkv_proj(h_by_dil[dil], w_qkv[:, gi * QKV_GROUP_WIDTH:(gi + 1) * QKV_GROUP_WIDTH])
            attn_parts.append(_attention(qkv, tables[gi]))
        x = _merge_out(h, x, mod3, attn_parts, w_rest, pool_w[l].astype(bf16), pool_scale[l],
                       w_attn_br[l].astype(bf16), w_pool_br[l].astype(bf16), w_out[l].astype(bf16),
                       final_g, apply_final_norm=(l == depth - 1))
    return x
```

```python
import functools
import math

import jax
import jax.numpy as jnp
import numpy as np
from jax import lax
from jax.experimental import pallas as pl
from jax.experimental.pallas import tpu as pltpu

HEAD_DIM = 64
HEADS_PER_GROUP = 8
ATTN_GROUPS = ((128, 1), (512, 4), (2048, 16))
N_GROUPS = len(ATTN_GROUPS)
ATTN_WIDTH = HEADS_PER_GROUP * HEAD_DIM
QKV_GROUP_WIDTH = 3 * ATTN_WIDTH
QKV_WIDTH = N_GROUPS * QKV_GROUP_WIDTH
QBLOCK = 128
ATTN_STEP_ROWS = 2048
ATTN_PIPELINE_LAG = 4
POOL_WINDOWS = (2, 4, 8, 16)
POOL_GROUPS = len(POOL_WINDOWS)
POOL_HALO = 16
NUM_BUCKETS = 32
MAX_DISTANCE = 2048
EPS = 1e-6
MASKED = -1e30
LOG2E = math.log2(math.e)
LN2 = math.log(2.0)
LANES = 128
VMEM_LIMIT_BYTES = 56 * 1024 * 1024


def _t5_bucket(n):
    max_exact = NUM_BUCKETS // 2
    nf = jnp.maximum(n, 1).astype(jnp.float32)
    large = max_exact + (jnp.log(nf / max_exact) / math.log(MAX_DISTANCE / max_exact)
                         * (NUM_BUCKETS - max_exact)).astype(jnp.int32)
    large = jnp.minimum(large, NUM_BUCKETS - 1)
    return jnp.where(n < max_exact, n, large)


def _bias_tables(bias_g, dil, n_back):
    i = jnp.arange(QBLOCK)[:, None]
    j = jnp.arange(2 * QBLOCK)[None, :]
    dist = QBLOCK + i - j
    ok = (dist >= 0) & (dist <= n_back)
    bucket = _t5_bucket(jnp.clip(dist, 0, n_back) * dil)
    onehot = (bucket[None] == jnp.arange(NUM_BUCKETS)[:, None, None]).astype(jnp.float32)
    bias = jnp.einsum('bh,bqk->hqk', bias_g.astype(jnp.float32), onehot,
                      precision=lax.Precision.HIGHEST)
    bias = bias * LOG2E
    full = jnp.where(ok[None], bias, MASKED)
    first = jnp.where((ok & (j >= QBLOCK))[None], bias, MASKED)
    return jnp.stack([full, first], axis=0)


def _mod_kernel(c_ref, w_ref, b_ref, o_ref):
    o_ref[...] = jnp.dot(c_ref[...], w_ref[...], preferred_element_type=jnp.float32,
                         precision=lax.Precision.HIGHEST) + b_ref[...]


def _modulation(c, w_ada, b_ada):
    B, D = c.shape
    N = w_ada.shape[1]
    tn = 1024
    return pl.pallas_call(
        _mod_kernel,
        grid=(N // tn,),
        in_specs=[pl.BlockSpec((B, D), lambda j: (0, 0)),
                  pl.BlockSpec((D, tn), lambda j: (0, j)),
                  pl.BlockSpec((1, tn), lambda j: (0, j))],
        out_specs=pl.BlockSpec((B, tn), lambda j: (0, j)),
        out_shape=jax.ShapeDtypeStruct((B, N), jnp.float32),
        name="adaln_mod",
    )(c, w_ada, b_ada.reshape(1, N))


def _prenorm_rows(x, g_ref, mod_ref):
    y = x * lax.rsqrt(jnp.mean(x * x, axis=-1, keepdims=True) + EPS)
    y = y * g_ref[...]
    return y * (1.0 + mod_ref[0, 1:2, :]) + mod_ref[0, 0:1, :]


def _qkv_kernel(x_ref, g_ref, mod_ref, w_ref, o_ref, *stage, dil, tt):
    h = _prenorm_rows(x_ref[0], g_ref, mod_ref)
    if dil > 1:
        (stage_ref,) = stage
        n_slabs = h.shape[-1] // LANES
        for cb in range(n_slabs):
            stage_ref[cb] = h[:, cb * LANES:(cb + 1) * LANES]
        h = jnp.concatenate(
            [jnp.concatenate([stage_ref[cb, pl.ds(r, tt // dil, stride=dil), :] for cb in range(n_slabs)], axis=-1)
             for r in range(dil)], axis=0)
    res = jnp.dot(h.astype(w_ref.dtype), w_ref[...], preferred_element_type=jnp.float32)
    o_ref[0] = res.astype(o_ref.dtype).reshape(o_ref.shape[1:])


def _qkv_proj(x, norm_g, mod3, w_qkv, dil):
    B, S, D = x.shape
    tt = 1024
    N = w_qkv.shape[1]
    kernel = functools.partial(_qkv_kernel, dil=dil, tt=tt)
    return pl.pallas_call(
        kernel,
        grid=(B, S // tt),
        in_specs=[pl.BlockSpec((1, tt, D), lambda b, i: (b, i, 0)),
                  pl.BlockSpec((1, D), lambda b, i: (0, 0)),
                  pl.BlockSpec((1, 3, D), lambda b, i: (b, 0, 0)),
                  pl.BlockSpec((D, N), lambda b, i: (0, 0))],
        out_specs=pl.BlockSpec((1, dil, tt // dil, N), lambda b, i: (b, 0, i, 0)),
        out_shape=jax.ShapeDtypeStruct((B, dil, S // dil, N), jnp.bfloat16),
        scratch_shapes=[pltpu.VMEM((D // LANES, tt, LANES), jnp.float32)] if dil > 1 else [],
        compiler_params=pltpu.CompilerParams(vmem_limit_bytes=VMEM_LIMIT_BYTES),
        name=f"qkv_proj_d{dil}",
    )(x, norm_g.reshape(1, D), mod3, w_qkv)


def _attn_kernel(q_ref, kc_ref, kp_ref, vc_ref, vp_ref, bias_ref, ones_ref, o_ref, lse_ref, s_ref, m_ref, *, rows):
    n = pl.program_id(2)
    first_sel = jnp.where(n == 0, 1, 0)
    lane = lax.broadcasted_iota(jnp.int32, (1, LANES), 1)
    head_lanes = [lane < HEAD_DIM, lane >= HEAD_DIM]
    n_heads = HEADS_PER_GROUP

    blocks_per_sub = rows // QBLOCK

    def pair_operands(ref_c, ref_p, rl, j, cs):
        if j == 0:
            return jnp.concatenate([ref_p[0, rl, :, cs], ref_c[0, rl, 0:QBLOCK, cs]], axis=0)
        return ref_c[0, rl, (j - 1) * QBLOCK:(j + 1) * QBLOCK, cs]

    def unit(u):
        blk, h = divmod(u, n_heads)
        rl, j = divmod(blk, blocks_per_sub)
        hp, e = divmod(h, 2)
        return rl, j, h, e, slice(hp * LANES, (hp + 1) * LANES)

    def logits_pass(u):
        rl, j, h, e, cs = unit(u)
        q_pair = q_ref[0, rl, j * QBLOCK:(j + 1) * QBLOCK, cs]
        k_pair = pair_operands(kc_ref, kp_ref, rl, j, cs)
        q_h = jnp.where(head_lanes[e], q_pair, jnp.zeros_like(q_pair))
        s = lax.dot_general(q_h, k_pair, (((1,), (1,)), ((), ())), preferred_element_type=jnp.float32)
        s = s + (bias_ref[first_sel, h] if j == 0 else bias_ref[0, h])
        s_ref[u % n_slots] = s
        m_ref[u % n_slots] = jnp.broadcast_to(jnp.max(s, axis=-1, keepdims=True), (QBLOCK, LANES))

    def values_pass(u0):
        rl, j, _, _, cs = unit(u0)
        v_pair = pair_operands(vc_ref, vp_ref, rl, j, cs)
        logits = [(s_ref[(u0 + e) % n_slots], m_ref[(u0 + e) % n_slots]) for e in range(2)]
        p_e, rhs_e = [], []
        for e, (s, m) in enumerate(logits):
            p_e += [jnp.exp2(s[:, :LANES] - m), jnp.exp2(s[:, LANES:] - m)]
            rhs_e.append(jnp.concatenate(
                [jnp.where(head_lanes[e], v_pair, jnp.zeros_like(v_pair)), ones_ref[e]], axis=-1))
        p = jnp.concatenate(p_e, axis=-1).astype(v_pair.dtype)
        acc = jnp.dot(p, jnp.concatenate(rhs_e, axis=0), preferred_element_type=jnp.float32)
        denom = acc[:, LANES:]
        qs = slice(j * QBLOCK, (j + 1) * QBLOCK)
        o_ref[0, rl, qs, cs] = (acc[:, :LANES] / denom).astype(o_ref.dtype)
        lse_ref[0, rl, qs, cs] = (jnp.where(head_lanes[0], logits[0][1], logits[1][1]) + jnp.log2(denom)) * LN2

    n_units = q_ref.shape[1] * blocks_per_sub * n_heads
    n_slots = s_ref.shape[0]
    lag = n_slots // 2
    for u in range(-lag, n_units, 2):
        if u + lag < n_units:
            logits_pass(u + lag)
            logits_pass(u + lag + 1)
        if u >= 0:
            values_pass(u)


def _attention(qkv, bias_tab):
    B, dil, L, _ = qkv.shape
    rows = min(ATTN_STEP_ROWS, L)
    nsub = min(dil, ATTN_STEP_ROWS // rows)
    rpb = rows // QBLOCK
    W = ATTN_WIDTH
    kernel = functools.partial(_attn_kernel, rows=rows)
    ones_cols = jnp.asarray(np.broadcast_to(
        (np.arange(LANES) // HEAD_DIM == np.arange(2)[:, None])[:, None, :], (2, 2 * QBLOCK, LANES)), qkv.dtype)
    cur = lambda col: pl.BlockSpec((1, nsub, rows, W), lambda b, r, n: (b, r, n, col))
    prev = lambda col: pl.BlockSpec((1, nsub, QBLOCK, W),
                                    lambda b, r, n: (b, r, jnp.maximum(n * rpb - 1, 0), col))
    out_spec = pl.BlockSpec((1, nsub, rows, W), lambda b, r, n: (b, r, n, 0))
    return pl.pallas_call(
        kernel,
        grid=(B, dil // nsub, L // rows),
        in_specs=[cur(0), cur(1), prev(1), cur(2), prev(2),
                  pl.BlockSpec((2, HEADS_PER_GROUP, QBLOCK, 2 * QBLOCK), lambda b, r, n: (0, 0, 0, 0)),
                  pl.BlockSpec((2, 2 * QBLOCK, LANES), lambda b, r, n: (0, 0, 0))],
        out_specs=[out_spec, out_spec],
        out_shape=[jax.ShapeDtypeStruct((B, dil, L, W), jnp.bfloat16),
                   jax.ShapeDtypeStruct((B, dil, L, W), jnp.float32)],
        scratch_shapes=[pltpu.VMEM((2 * ATTN_PIPELINE_LAG, QBLOCK, 2 * QBLOCK), jnp.float32),
                        pltpu.VMEM((2 * ATTN_PIPELINE_LAG, QBLOCK, LANES), jnp.float32)],
        compiler_params=pltpu.CompilerParams(vmem_limit_bytes=VMEM_LIMIT_BYTES),
        name=f"band_attn_d{dil}",
    )(qkv, qkv, qkv, qkv, qkv, bias_tab, ones_cols)


def _sigmoid(v):
    return 1.0 / (1.0 + jnp.exp(-v))


def _token_order(ref, stage_ref, tm):
    d = ref.shape[1]
    if d == 1:
        return ref[0, 0].astype(jnp.float32)
    n_slabs = ref.shape[-1] // LANES
    for r in range(d):
        for cb in range(n_slabs):
            stage_ref[cb, pl.ds(r, tm // d, stride=d), :] = (
                ref[0, r, :, cb * LANES:(cb + 1) * LANES].astype(jnp.float32))
    return jnp.concatenate([stage_ref[cb] for cb in range(n_slabs)], axis=-1)


def _out_kernel(x_ref, xprev_ref, g_ref, mod_ref,
                o0_ref, o1_ref, o2_ref, l0_ref, l1_ref, l2_ref,
                w_rest_ref, pool_w_ref, pool_scale_ref, w_attn_ref, w_pool_ref, w_out_ref, fin_ref,
                out_ref, ext_ref, so1_ref, so2_ref, sl1_ref, sl2_ref, *, tm, apply_final_norm):
    i = pl.program_id(1)
    W = ATTN_WIDTH
    h = _prenorm_rows(x_ref[0], g_ref, mod_ref).astype(jnp.bfloat16)
    proj = lambda lo, hi: jnp.dot(h, w_rest_ref[:, lo:hi], preferred_element_type=jnp.float32)
    z_attn = proj(0, W)
    u_pool = proj(W, 2 * W)
    z_pool = proj(2 * W, 3 * W)

    l0 = _token_order(l0_ref, None, tm)
    l1 = _token_order(l1_ref, sl1_ref, tm)
    l2 = _token_order(l2_ref, sl2_ref, tm)
    mx = jnp.maximum(jnp.maximum(l0, l1), l2)
    e0, e1, e2 = jnp.exp(l0 - mx), jnp.exp(l1 - mx), jnp.exp(l2 - mx)
    num = (e0 * _token_order(o0_ref, None, tm) + e1 * _token_order(o1_ref, so1_ref, tm)
           + e2 * _token_order(o2_ref, so2_ref, tm))
    attn = num / (e0 + e1 + e2)
    y_attn = jnp.dot((attn * (z_attn * _sigmoid(z_attn))).astype(jnp.bfloat16), w_attn_ref[...],
                     preferred_element_type=jnp.float32)

    h_prev = _prenorm_rows(xprev_ref[0], g_ref, mod_ref).astype(jnp.bfloat16)
    u_prev = jnp.dot(h_prev, w_rest_ref[:, W:2 * W], preferred_element_type=jnp.float32)
    u_prev = jnp.where(i == 0, 0.0, u_prev)
    ext_ref[0:POOL_HALO, :] = u_prev
    ext_ref[POOL_HALO:POOL_HALO + tm, :] = u_pool
    t = i * tm + lax.broadcasted_iota(jnp.int32, (tm, 1), 0)
    mixed = []
    for g, win in enumerate(POOL_WINDOWS):
        cs = slice(g * LANES, (g + 1) * LANES)
        s = ext_ref[POOL_HALO:POOL_HALO + tm, cs]
        for back in range(1, win):
            s = s + ext_ref[POOL_HALO - back:POOL_HALO - back + tm, cs]
        cnt = jnp.minimum(t + 1, win).astype(jnp.float32)
        pooled = s / cnt - u_pool[:, cs]
        mixed.append(jnp.dot(pooled.astype(jnp.bfloat16), pool_w_ref[g],
                             preferred_element_type=jnp.float32))
    mixed = jnp.concatenate(mixed, axis=-1) * pool_scale_ref[...]
    y_pool = jnp.dot((mixed * (z_pool * _sigmoid(z_pool))).astype(jnp.bfloat16), w_pool_ref[...],
                     preferred_element_type=jnp.float32)

    merged = _sigmoid(proj(3 * W, 5 * W)) * y_attn + _sigmoid(proj(5 * W, 7 * W)) * y_pool
    gate = mod_ref[0, 2:3, :]
    y = x_ref[0] + gate * jnp.dot(merged.astype(jnp.bfloat16), w_out_ref[...],
                                  preferred_element_type=jnp.float32)
    if apply_final_norm:
        y = y * lax.rsqrt(jnp.mean(y * y, axis=-1, keepdims=True) + EPS) * fin_ref[...]
    out_ref[0] = y


def _merge_out(x, norm_g, mod3, attn_parts, w_rest, pool_w, pool_scale, w_attn_br, w_pool_br, w_out,
               final_g, apply_final_norm):
    B, S, D = x.shape
    tm = 512
    W = ATTN_WIDTH
    hb = tm // POOL_HALO
    tok = lambda width: pl.BlockSpec((1, tm, width), lambda b, i: (b, i, 0))
    const = lambda shape: pl.BlockSpec(shape, lambda b, i: (0,) * len(shape))
    (o0, l0), (o1, l1), (o2, l2) = attn_parts
    sub = lambda a: pl.BlockSpec((1, a.shape[1], tm // a.shape[1], W), lambda b, i: (b, 0, i, 0))
    stage = pltpu.VMEM((W // LANES, tm, LANES), jnp.float32)
    kernel = functools.partial(_out_kernel, tm=tm, apply_final_norm=apply_final_norm)
    return pl.pallas_call(
        kernel,
        grid=(B, S // tm),
        in_specs=[tok(D),
                  pl.BlockSpec((1, POOL_HALO, D), lambda b, i: (b, jnp.maximum(i * hb - 1, 0), 0)),
                  const((1, D)),
                  pl.BlockSpec((1, 3, D), lambda b, i: (b, 0, 0)),
                  sub(o0), sub(o1), sub(o2), sub(l0), sub(l1), sub(l2),
                  const(w_rest.shape), const(pool_w.shape), const((1, POOL_GROUPS * LANES)),
                  const(w_attn_br.shape), const(w_pool_br.shape), const(w_out.shape), const((1, D))],
        out_specs=tok(D),
        out_shape=jax.ShapeDtypeStruct((B, S, D), x.dtype),
        scratch_shapes=[pltpu.VMEM((tm + POOL_HALO, POOL_GROUPS * LANES), jnp.float32),
                        stage, stage, stage, stage],
        compiler_params=pltpu.CompilerParams(vmem_limit_bytes=VMEM_LIMIT_BYTES),
        name="merge_out",
    )(x, x, norm_g.reshape(1, D), mod3, o0, o1, o2, l0, l1, l2,
      w_rest, pool_w, pool_scale.reshape(1, -1), w_attn_br, w_pool_br, w_out, final_g.reshape(1, D))


def kernel(x, c, norm_g, w_ada, b_ada, w_in, pool_w, pool_scale, w_attn_br, w_pool_br, w_out, rel_bias, final_g):
    B, S, D = x.shape
    depth = norm_g.shape[0]
    bf16 = jnp.bfloat16
    col = np.arange(QKV_WIDTH) % QKV_GROUP_WIDTH
    q_scale = jnp.asarray(np.where(col < ATTN_WIDTH, HEAD_DIM ** -0.5 * LOG2E, 1.0), jnp.float32)
    tables = []
    for gi, (win, dil) in enumerate(ATTN_GROUPS):
        assert win // dil <= QBLOCK and S % (dil * QBLOCK) == 0
        tables.append(_bias_tables(rel_bias[:, gi * HEADS_PER_GROUP:(gi + 1) * HEADS_PER_GROUP], dil, win // dil))

    for l in range(depth):
        w_qkv = (w_in[l, :, :QKV_WIDTH] * q_scale).astype(bf16)
        w_rest = w_in[l, :, QKV_WIDTH:].astype(bf16)
        mod3 = _modulation(c, w_ada[l], b_ada[l]).reshape(B, 3, D)
        attn_parts = []
        for gi, (win, dil) in enumerate(ATTN_GROUPS):
            qkv = _qkv_proj(x, norm_g[l], mod3, w_qkv[:, gi * QKV_GROUP_WIDTH:(gi + 1) * QKV_GROUP_WIDTH], dil)
            attn_parts.append(_attention(qkv, tables[gi]))
        x = _merge_out(x, norm_g[l], mod3, attn_parts, w_rest, pool_w[l].astype(bf16), pool_scale[l],
                       w_attn_br[l].astype(bf16), w_pool_br[l].astype(bf16), w_out[l].astype(bf16),
                       final_g, apply_final_norm=(l == depth - 1))
    return x
```

```python
import functools
import math

import jax
import jax.numpy as jnp
import numpy as np
from jax import lax
from jax.experimental import pallas as pl
from jax.experimental.pallas import tpu as pltpu

HEAD_DIM = 64
HEADS_PER_GROUP = 8
ATTN_GROUPS = ((128, 1), (512, 4), (2048, 16))
N_GROUPS = len(ATTN_GROUPS)
ATTN_WIDTH = HEADS_PER_GROUP * HEAD_DIM
QKV_GROUP_WIDTH = 3 * ATTN_WIDTH
QKV_WIDTH = N_GROUPS * QKV_GROUP_WIDTH
QBLOCK = 128
ATTN_STEP_ROWS = 2048
ATTN_PIPELINE_LAG = 4
POOL_WINDOWS = (2, 4, 8, 16)
POOL_GROUPS = len(POOL_WINDOWS)
POOL_HALO = 16
NUM_BUCKETS = 32
MAX_DISTANCE = 2048
EPS = 1e-6
MASKED = -1e30
LOG2E = math.log2(math.e)
LN2 = math.log(2.0)
LANES = 128
VMEM_LIMIT_BYTES = 56 * 1024 * 1024


def _t5_bucket(n):
    max_exact = NUM_BUCKETS // 2
    nf = jnp.maximum(n, 1).astype(jnp.float32)
    large = max_exact + (jnp.log(nf / max_exact) / math.log(MAX_DISTANCE / max_exact)
                         * (NUM_BUCKETS - max_exact)).astype(jnp.int32)
    large = jnp.minimum(large, NUM_BUCKETS - 1)
    return jnp.where(n < max_exact, n, large)


def _bias_tables(bias_g, dil, n_back):
    i = jnp.arange(QBLOCK)[:, None]
    j = jnp.arange(2 * QBLOCK)[None, :]
    dist = QBLOCK + i - j
    ok = (dist >= 0) & (dist <= n_back)
    bucket = _t5_bucket(jnp.clip(dist, 0, n_back) * dil)
    onehot = (bucket[None] == jnp.arange(NUM_BUCKETS)[:, None, None]).astype(jnp.float32)
    bias = jnp.einsum('bh,bqk->hqk', bias_g.astype(jnp.float32), onehot,
                      precision=lax.Precision.HIGHEST)
    bias = bias * LOG2E
    full = jnp.where(ok[None], bias, MASKED)
    first = jnp.where((ok & (j >= QBLOCK))[None], bias, MASKED)
    return jnp.stack([full, first], axis=0)


def _mod_kernel(c_ref, w_ref, b_ref, o_ref):
    o_ref[...] = jnp.dot(c_ref[...], w_ref[...], preferred_element_type=jnp.float32,
                         precision=lax.Precision.HIGHEST) + b_ref[...]


def _modulation(c, w_ada, b_ada):
    B, D = c.shape
    N = w_ada.shape[1]
    tn = 1024
    return pl.pallas_call(
        _mod_kernel,
        grid=(N // tn,),
        in_specs=[pl.BlockSpec((B, D), lambda j: (0, 0)),
                  pl.BlockSpec((D, tn), lambda j: (0, j)),
                  pl.BlockSpec((1, tn), lambda j: (0, j))],
        out_specs=pl.BlockSpec((B, tn), lambda j: (0, j)),
        out_shape=jax.ShapeDtypeStruct((B, N), jnp.float32),
        name="adaln_mod",
    )(c, w_ada, b_ada.reshape(1, N))


def _prenorm_rows(x, g_ref, mod_ref):
    y = x * lax.rsqrt(jnp.mean(x * x, axis=-1, keepdims=True) + EPS)
    y = y * g_ref[...]
    return y * (1.0 + mod_ref[0, 1:2, :]) + mod_ref[0, 0:1, :]


def _qkv_kernel(x_ref, g_ref, mod_ref, w_ref, o_ref, *stage, dil, tt, tc):
    rows = tc // dil

    def regroup(c):
        h = _prenorm_rows(x_ref[0, c * tc:(c + 1) * tc, :], g_ref, mod_ref)
        if dil > 1:
            (stage_ref,) = stage
            n_slabs = h.shape[-1] // LANES
            for cb in range(n_slabs):
                stage_ref[c % 2, cb] = h[:, cb * LANES:(cb + 1) * LANES]
            h = jnp.concatenate(
                [jnp.concatenate([stage_ref[c % 2, cb, pl.ds(r, rows, stride=dil), :] for cb in range(n_slabs)],
                                 axis=-1) for r in range(dil)], axis=0)
        return h.astype(w_ref.dtype)

    def project(c, h):
        res = jnp.dot(h, w_ref[...], preferred_element_type=jnp.float32).astype(o_ref.dtype)
        for r in range(dil):
            o_ref[0, r, c * rows:(c + 1) * rows, :] = res[r * rows:(r + 1) * rows]

    h_next = regroup(0)
    for c in range(tt // tc):
        h_cur = h_next
        if (c + 1) * tc < tt:
            h_next = regroup(c + 1)
        project(c, h_cur)


def _qkv_proj(x, norm_g, mod3, w_qkv, dil):
    B, S, D = x.shape
    tt = 1024
    tc = 256
    N = w_qkv.shape[1]
    kernel = functools.partial(_qkv_kernel, dil=dil, tt=tt, tc=tc)
    return pl.pallas_call(
        kernel,
        grid=(B, S // tt),
        in_specs=[pl.BlockSpec((1, tt, D), lambda b, i: (b, i, 0)),
                  pl.BlockSpec((1, D), lambda b, i: (0, 0)),
                  pl.BlockSpec((1, 3, D), lambda b, i: (b, 0, 0)),
                  pl.BlockSpec((D, N), lambda b, i: (0, 0))],
        out_specs=pl.BlockSpec((1, dil, tt // dil, N), lambda b, i: (b, 0, i, 0)),
        out_shape=jax.ShapeDtypeStruct((B, dil, S // dil, N), jnp.bfloat16),
        scratch_shapes=[pltpu.VMEM((2, D // LANES, tc, LANES), jnp.float32)] if dil > 1 else [],
        compiler_params=pltpu.CompilerParams(vmem_limit_bytes=VMEM_LIMIT_BYTES),
        name=f"qkv_proj_d{dil}",
    )(x, norm_g.reshape(1, D), mod3, w_qkv)


def _attn_kernel(q_ref, kc_ref, kp_ref, vc_ref, vp_ref, bias_ref, ones_ref, o_ref, lse_ref, s_ref, m_ref, *, rows):
    n = pl.program_id(2)
    first_sel = jnp.where(n == 0, 1, 0)
    lane = lax.broadcasted_iota(jnp.int32, (1, LANES), 1)
    head_lanes = [lane < HEAD_DIM, lane >= HEAD_DIM]
    n_heads = HEADS_PER_GROUP

    blocks_per_sub = rows // QBLOCK

    def pair_operands(ref_c, ref_p, rl, j, cs):
        if j == 0:
            return jnp.concatenate([ref_p[0, rl, :, cs], ref_c[0, rl, 0:QBLOCK, cs]], axis=0)
        return ref_c[0, rl, (j - 1) * QBLOCK:(j + 1) * QBLOCK, cs]

    def unit(u):
        blk, h = divmod(u, n_heads)
        rl, j = divmod(blk, blocks_per_sub)
        hp, e = divmod(h, 2)
        return rl, j, h, e, slice(hp * LANES, (hp + 1) * LANES)

    def logits_pass(u):
        rl, j, h, e, cs = unit(u)
        q_pair = q_ref[0, rl, j * QBLOCK:(j + 1) * QBLOCK, cs]
        k_pair = pair_operands(kc_ref, kp_ref, rl, j, cs)
        q_h = jnp.where(head_lanes[e], q_pair, jnp.zeros_like(q_pair))
        s = lax.dot_general(q_h, k_pair, (((1,), (1,)), ((), ())), preferred_element_type=jnp.float32)
        s = s + (bias_ref[first_sel, h] if j == 0 else bias_ref[0, h])
        s_ref[u % n_slots] = s
        m_ref[u % n_slots] = jnp.broadcast_to(jnp.max(s, axis=-1, keepdims=True), (QBLOCK, LANES))

    def values_pass(u0):
        rl, j, _, _, cs = unit(u0)
        v_pair = pair_operands(vc_ref, vp_ref, rl, j, cs)
        logits = [(s_ref[(u0 + e) % n_slots], m_ref[(u0 + e) % n_slots]) for e in range(2)]
        p_e, rhs_e = [], []
        for e, (s, m) in enumerate(logits):
            p_e += [jnp.exp2(s[:, :LANES] - m), jnp.exp2(s[:, LANES:] - m)]
            rhs_e.append(jnp.concatenate(
                [jnp.where(head_lanes[e], v_pair, jnp.zeros_like(v_pair)), ones_ref[e]], axis=-1))
        p = jnp.concatenate(p_e, axis=-1).astype(v_pair.dtype)
        acc = jnp.dot(p, jnp.concatenate(rhs_e, axis=0), preferred_element_type=jnp.float32)
        denom = acc[:, LANES:]
        qs = slice(j * QBLOCK, (j + 1) * QBLOCK)
        o_ref[0, rl, qs, cs] = (acc[:, :LANES] / denom).astype(o_ref.dtype)
        lse_ref[0, rl, qs, cs] = (jnp.where(head_lanes[0], logits[0][1], logits[1][1]) + jnp.log2(denom)) * LN2

    n_units = q_ref.shape[1] * blocks_per_sub * n_heads
    n_slots = s_ref.shape[0]
    lag = n_slots // 2
    for u in range(-lag, n_units, 2):
        if u + lag < n_units:
            logits_pass(u + lag)
            logits_pass(u + lag + 1)
        if u >= 0:
            values_pass(u)


def _attention(qkv, bias_tab):
    B, dil, L, _ = qkv.shape
    rows = min(ATTN_STEP_ROWS, L)
    nsub = min(dil, ATTN_STEP_ROWS // rows)
    rpb = rows // QBLOCK
    W = ATTN_WIDTH
    kernel = functools.partial(_attn_kernel, rows=rows)
    ones_cols = jnp.asarray(np.broadcast_to(
        (np.arange(LANES) // HEAD_DIM == np.arange(2)[:, None])[:, None, :], (2, 2 * QBLOCK, LANES)), qkv.dtype)
    cur = lambda col: pl.BlockSpec((1, nsub, rows, W), lambda b, r, n: (b, r, n, col))
    prev = lambda col: pl.BlockSpec((1, nsub, QBLOCK, W),
                                    lambda b, r, n: (b, r, jnp.maximum(n * rpb - 1, 0), col))
    out_spec = pl.BlockSpec((1, nsub, rows, W), lambda b, r, n: (b, r, n, 0))
    return pl.pallas_call(
        kernel,
        grid=(B, dil // nsub, L // rows),
        in_specs=[cur(0), cur(1), prev(1), cur(2), prev(2),
                  pl.BlockSpec((2, HEADS_PER_GROUP, QBLOCK, 2 * QBLOCK), lambda b, r, n: (0, 0, 0, 0)),
                  pl.BlockSpec((2, 2 * QBLOCK, LANES), lambda b, r, n: (0, 0, 0))],
        out_specs=[out_spec, out_spec],
        out_shape=[jax.ShapeDtypeStruct((B, dil, L, W), jnp.bfloat16),
                   jax.ShapeDtypeStruct((B, dil, L, W), jnp.float32)],
        scratch_shapes=[pltpu.VMEM((2 * ATTN_PIPELINE_LAG, QBLOCK, 2 * QBLOCK), jnp.float32),
                        pltpu.VMEM((2 * ATTN_PIPELINE_LAG, QBLOCK, LANES), jnp.float32)],
        compiler_params=pltpu.CompilerParams(vmem_limit_bytes=VMEM_LIMIT_BYTES),
        name=f"band_attn_d{dil}",
    )(qkv, qkv, qkv, qkv, qkv, bias_tab, ones_cols)


def _sigmoid(v):
    return 1.0 / (1.0 + jnp.exp(-v))


def _token_order(ref, stage_ref, tm):
    d = ref.shape[1]
    if d == 1:
        return ref[0, 0].astype(jnp.float32)
    n_slabs = ref.shape[-1] // LANES
    for r in range(d):
        for cb in range(n_slabs):
            stage_ref[cb, pl.ds(r, tm // d, stride=d), :] = (
                ref[0, r, :, cb * LANES:(cb + 1) * LANES].astype(jnp.float32))
    return jnp.concatenate([stage_ref[cb] for cb in range(n_slabs)], axis=-1)


def _out_kernel(x_ref, xprev_ref, g_ref, mod_ref,
                o0_ref, o1_ref, o2_ref, l0_ref, l1_ref, l2_ref,
                w_rest_ref, pool_w_ref, pool_scale_ref, w_attn_ref, w_pool_ref, w_out_ref, fin_ref,
                out_ref, ext_ref, so1_ref, so2_ref, sl1_ref, sl2_ref, *, tm, apply_final_norm):
    i = pl.program_id(1)
    W = ATTN_WIDTH
    h = _prenorm_rows(x_ref[0], g_ref, mod_ref).astype(jnp.bfloat16)
    proj = lambda lo, hi: jnp.dot(h, w_rest_ref[:, lo:hi], preferred_element_type=jnp.float32)
    u_pool = proj(W, 2 * W)
    h_prev = _prenorm_rows(xprev_ref[0], g_ref, mod_ref).astype(jnp.bfloat16)
    u_prev = jnp.dot(h_prev, w_rest_ref[:, W:2 * W], preferred_element_type=jnp.float32)
    u_prev = jnp.where(i == 0, 0.0, u_prev)
    ext_ref[0:POOL_HALO, :] = u_prev
    ext_ref[POOL_HALO:POOL_HALO + tm, :] = u_pool

    l0 = _token_order(l0_ref, None, tm)
    l1 = _token_order(l1_ref, sl1_ref, tm)
    l2 = _token_order(l2_ref, sl2_ref, tm)
    z_pool = proj(2 * W, 3 * W)
    mx = jnp.maximum(jnp.maximum(l0, l1), l2)
    e0, e1, e2 = jnp.exp(l0 - mx), jnp.exp(l1 - mx), jnp.exp(l2 - mx)
    z_attn = proj(0, W)
    num = (e0 * _token_order(o0_ref, None, tm) + e1 * _token_order(o1_ref, so1_ref, tm)
           + e2 * _token_order(o2_ref, so2_ref, tm))
    attn = num / (e0 + e1 + e2)
    g_pool = proj(5 * W, 7 * W)

    t = i * tm + lax.broadcasted_iota(jnp.int32, (tm, 1), 0)
    mixed = []
    for g, win in enumerate(POOL_WINDOWS):
        cs = slice(g * LANES, (g + 1) * LANES)
        s = ext_ref[POOL_HALO:POOL_HALO + tm, cs]
        for back in range(1, win):
            s = s + ext_ref[POOL_HALO - back:POOL_HALO - back + tm, cs]
        cnt = jnp.minimum(t + 1, win).astype(jnp.float32)
        pooled = s / cnt - u_pool[:, cs]
        mixed.append(jnp.dot(pooled.astype(jnp.bfloat16), pool_w_ref[g],
                             preferred_element_type=jnp.float32))
    g_attn = proj(3 * W, 5 * W)
    mixed = jnp.concatenate(mixed, axis=-1) * pool_scale_ref[...]
    y_pool = jnp.dot((mixed * (z_pool * _sigmoid(z_pool))).astype(jnp.bfloat16), w_pool_ref[...],
                     preferred_element_type=jnp.float32)
    y_attn = jnp.dot((attn * (z_attn * _sigmoid(z_attn))).astype(jnp.bfloat16), w_attn_ref[...],
                     preferred_element_type=jnp.float32)
    merged = _sigmoid(g_attn) * y_attn + _sigmoid(g_pool) * y_pool
    gate = mod_ref[0, 2:3, :]
    y = x_ref[0] + gate * jnp.dot(merged.astype(jnp.bfloat16), w_out_ref[...],
                                  preferred_element_type=jnp.float32)
    if apply_final_norm:
        y = y * lax.rsqrt(jnp.mean(y * y, axis=-1, keepdims=True) + EPS) * fin_ref[...]
    out_ref[0] = y


def _merge_out(x, norm_g, mod3, attn_parts, w_rest, pool_w, pool_scale, w_attn_br, w_pool_br, w_out,
               final_g, apply_final_norm):
    B, S, D = x.shape
    tm = 512
    W = ATTN_WIDTH
    hb = tm // POOL_HALO
    tok = lambda width: pl.BlockSpec((1, tm, width), lambda b, i: (b, i, 0))
    const = lambda shape: pl.BlockSpec(shape, lambda b, i: (0,) * len(shape))
    (o0, l0), (o1, l1), (o2, l2) = attn_parts
    sub = lambda a: pl.BlockSpec((1, a.shape[1], tm // a.shape[1], W), lambda b, i: (b, 0, i, 0))
    stage = pltpu.VMEM((W // LANES, tm, LANES), jnp.float32)
    kernel = functools.partial(_out_kernel, tm=tm, apply_final_norm=apply_final_norm)
    return pl.pallas_call(
        kernel,
        grid=(B, S // tm),
        in_specs=[tok(D),
                  pl.BlockSpec((1, POOL_HALO, D), lambda b, i: (b, jnp.maximum(i * hb - 1, 0), 0)),
                  const((1, D)),
                  pl.BlockSpec((1, 3, D), lambda b, i: (b, 0, 0)),
                  sub(o0), sub(o1), sub(o2), sub(l0), sub(l1), sub(l2),
                  const(w_rest.shape), const(pool_w.shape), const((1, POOL_GROUPS * LANES)),
                  const(w_attn_br.shape), const(w_pool_br.shape), const(w_out.shape), const((1, D))],
        out_specs=tok(D),
        out_shape=jax.ShapeDtypeStruct((B, S, D), x.dtype),
        scratch_shapes=[pltpu.VMEM((tm + POOL_HALO, POOL_GROUPS * LANES), jnp.float32),
                        stage, stage, stage, stage],
        compiler_params=pltpu.CompilerParams(vmem_limit_bytes=VMEM_LIMIT_BYTES),
        name="merge_out",
    )(x, x, norm_g.reshape(1, D), mod3, o0, o1, o2, l0, l1, l2,
      w_rest, pool_w, pool_scale.reshape(1, -1), w_attn_br, w_pool_br, w_out, final_g.reshape(1, D))


def kernel(x, c, norm_g, w_ada, b_ada, w_in, pool_w, pool_scale, w_attn_br, w_pool_br, w_out, rel_bias, final_g):
    B, S, D = x.shape
    depth = norm_g.shape[0]
    bf16 = jnp.bfloat16
    col = np.arange(QKV_WIDTH) % QKV_GROUP_WIDTH
    q_scale = jnp.asarray(np.where(col < ATTN_WIDTH, HEAD_DIM ** -0.5 * LOG2E, 1.0), jnp.float32)
    tables = []
    for gi, (win, dil) in enumerate(ATTN_GROUPS):
        assert win // dil <= QBLOCK and S % (dil * QBLOCK) == 0
        tables.append(_bias_tables(rel_bias[:, gi * HEADS_PER_GROUP:(gi + 1) * HEADS_PER_GROUP], dil, win // dil))

    for l in range(depth):
        w_qkv = (w_in[l, :, :QKV_WIDTH] * q_scale).astype(bf16)
        w_rest = w_in[l, :, QKV_WIDTH:].astype(bf16)
        mod3 = _modulation(c, w_ada[l], b_ada[l]).reshape(B, 3, D)
        attn_parts = []
        for gi, (win, dil) in enumerate(ATTN_GROUPS):
            qkv = _qkv_proj(x, norm_g[l], mod3, w_qkv[:, gi * QKV_GROUP_WIDTH:(gi + 1) * QKV_GROUP_WIDTH], dil)
            attn_parts.append(_attention(qkv, tables[gi]))
        x = _merge_out(x, norm_g[l], mod3, attn_parts, w_rest, pool_w[l].astype(bf16), pool_scale[l],
                       w_attn_br[l].astype(bf16), w_pool_br[l].astype(bf16), w_out[l].astype(bf16),
                       final_g, apply_final_norm=(l == depth - 1))
    return x
```

```python
import functools
import math

import jax
import jax.numpy as jnp
import numpy as np
from jax import lax
from jax.experimental import pallas as pl
from jax.experimental.pallas import tpu as pltpu

HEAD_DIM = 64
HEADS_PER_GROUP = 8
ATTN_GROUPS = ((128, 1), (512, 4), (2048, 16))
N_GROUPS = len(ATTN_GROUPS)
ATTN_WIDTH = HEADS_PER_GROUP * HEAD_DIM
QKV_GROUP_WIDTH = 3 * ATTN_WIDTH
QKV_WIDTH = N_GROUPS * QKV_GROUP_WIDTH
QBLOCK = 128
ATTN_STEP_ROWS = 2048
ATTN_PIPELINE_LAG = 4
POOL_WINDOWS = (2, 4, 8, 16)
POOL_GROUPS = len(POOL_WINDOWS)
POOL_HALO = 16
NUM_BUCKETS = 32
MAX_DISTANCE = 2048
EPS = 1e-6
MASKED = -1e30
LOG2E = math.log2(math.e)
LN2 = math.log(2.0)
LANES = 128
VMEM_LIMIT_BYTES = 56 * 1024 * 1024


def _t5_bucket(n):
    max_exact = NUM_BUCKETS // 2
    nf = jnp.maximum(n, 1).astype(jnp.float32)
    large = max_exact + (jnp.log(nf / max_exact) / math.log(MAX_DISTANCE / max_exact)
                         * (NUM_BUCKETS - max_exact)).astype(jnp.int32)
    large = jnp.minimum(large, NUM_BUCKETS - 1)
    return jnp.where(n < max_exact, n, large)


def _bias_tables(bias_g, dil, n_back):
    i = jnp.arange(QBLOCK)[:, None]
    j = jnp.arange(2 * QBLOCK)[None, :]
    dist = QBLOCK + i - j
    ok = (dist >= 0) & (dist <= n_back)
    bucket = _t5_bucket(jnp.clip(dist, 0, n_back) * dil)
    onehot = (bucket[None] == jnp.arange(NUM_BUCKETS)[:, None, None]).astype(jnp.float32)
    bias = jnp.einsum('bh,bqk->hqk', bias_g.astype(jnp.float32), onehot,
                      precision=lax.Precision.HIGHEST)
    bias = bias * LOG2E
    full = jnp.where(ok[None], bias, MASKED)
    first = jnp.where((ok & (j >= QBLOCK))[None], bias, MASKED)
    return jnp.stack([full, first], axis=0)


def _mod_kernel(c_ref, w_ref, b_ref, o_ref):
    o_ref[...] = jnp.dot(c_ref[...], w_ref[...], preferred_element_type=jnp.float32,
                         precision=lax.Precision.HIGHEST) + b_ref[...]


def _modulation(c, w_ada, b_ada):
    B, D = c.shape
    N = w_ada.shape[1]
    tn = 1024
    return pl.pallas_call(
        _mod_kernel,
        grid=(N // tn,),
        in_specs=[pl.BlockSpec((B, D), lambda j: (0, 0)),
                  pl.BlockSpec((D, tn), lambda j: (0, j)),
                  pl.BlockSpec((1, tn), lambda j: (0, j))],
        out_specs=pl.BlockSpec((B, tn), lambda j: (0, j)),
        out_shape=jax.ShapeDtypeStruct((B, N), jnp.float32),
        name="adaln_mod",
    )(c, w_ada, b_ada.reshape(1, N))


def _prenorm_rows(x, g_ref, mod_ref):
    y = x * lax.rsqrt(jnp.mean(x * x, axis=-1, keepdims=True) + EPS)
    y = y * g_ref[...]
    return y * (1.0 + mod_ref[0, 1:2, :]) + mod_ref[0, 0:1, :]


def _qkv_kernel(x_ref, g_ref, mod_ref, w_ref, o_ref, *stage, dil, tt, tc):
    rows = tc // dil

    def regroup(c):
        h = _prenorm_rows(x_ref[0, c * tc:(c + 1) * tc, :], g_ref, mod_ref)
        if dil > 1:
            (stage_ref,) = stage
            n_slabs = h.shape[-1] // LANES
            for cb in range(n_slabs):
                stage_ref[c % 2, cb] = h[:, cb * LANES:(cb + 1) * LANES]
            h = jnp.concatenate(
                [jnp.concatenate([stage_ref[c % 2, cb, pl.ds(r, rows, stride=dil), :] for cb in range(n_slabs)],
                                 axis=-1) for r in range(dil)], axis=0)
        return h.astype(w_ref.dtype)

    def project(c, h):
        res = jnp.dot(h, w_ref[...], preferred_element_type=jnp.float32).astype(o_ref.dtype)
        for r in range(dil):
            o_ref[0, r, c * rows:(c + 1) * rows, :] = res[r * rows:(r + 1) * rows]

    h_next = regroup(0)
    for c in range(tt // tc):
        h_cur = h_next
        if (c + 1) * tc < tt:
            h_next = regroup(c + 1)
        project(c, h_cur)


def _qkv_proj(x, norm_g, mod3, w_qkv, dil):
    B, S, D = x.shape
    tt = 1024
    tc = 256
    N = w_qkv.shape[1]
    kernel = functools.partial(_qkv_kernel, dil=dil, tt=tt, tc=tc)
    return pl.pallas_call(
        kernel,
        grid=(B, S // tt),
        in_specs=[pl.BlockSpec((1, tt, D), lambda b, i: (b, i, 0)),
                  pl.BlockSpec((1, D), lambda b, i: (0, 0)),
                  pl.BlockSpec((1, 3, D), lambda b, i: (b, 0, 0)),
                  pl.BlockSpec((D, N), lambda b, i: (0, 0))],
        out_specs=pl.BlockSpec((1, dil, tt // dil, N), lambda b, i: (b, 0, i, 0)),
        out_shape=jax.ShapeDtypeStruct((B, dil, S // dil, N), jnp.bfloat16),
        scratch_shapes=[pltpu.VMEM((2, D // LANES, tc, LANES), jnp.float32)] if dil > 1 else [],
        compiler_params=pltpu.CompilerParams(vmem_limit_bytes=VMEM_LIMIT_BYTES),
        name=f"qkv_proj_d{dil}",
    )(x, norm_g.reshape(1, D), mod3, w_qkv)


def _attn_kernel(q_ref, kc_ref, kp_ref, vc_ref, vp_ref, bias_ref, ones_ref, o_ref, lse_ref, s_ref, m_ref, *, rows):
    n = pl.program_id(2)
    first_sel = jnp.where(n == 0, 1, 0)
    lane = lax.broadcasted_iota(jnp.int32, (1, LANES), 1)
    head_lanes = [lane < HEAD_DIM, lane >= HEAD_DIM]
    n_heads = HEADS_PER_GROUP

    blocks_per_sub = rows // QBLOCK

    def pair_operands(ref_c, ref_p, rl, j, cs):
        if j == 0:
            return jnp.concatenate([ref_p[0, rl, :, cs], ref_c[0, rl, 0:QBLOCK, cs]], axis=0)
        return ref_c[0, rl, (j - 1) * QBLOCK:(j + 1) * QBLOCK, cs]

    def unit(u):
        blk, h = divmod(u, n_heads)
        rl, j = divmod(blk, blocks_per_sub)
        hp, e = divmod(h, 2)
        return rl, j, h, e, slice(hp * LANES, (hp + 1) * LANES)

    def logits_pass(u):
        rl, j, h, e, cs = unit(u)
        q_pair = q_ref[0, rl, j * QBLOCK:(j + 1) * QBLOCK, cs]
        k_pair = pair_operands(kc_ref, kp_ref, rl, j, cs)
        q_h = jnp.where(head_lanes[e], q_pair, jnp.zeros_like(q_pair))
        s = lax.dot_general(q_h, k_pair, (((1,), (1,)), ((), ())), preferred_element_type=jnp.float32)
        s = s + (bias_ref[first_sel, h] if j == 0 else bias_ref[0, h])
        s_ref[u % n_slots] = s
        m_ref[u % n_slots] = jnp.broadcast_to(jnp.max(s, axis=-1, keepdims=True), (QBLOCK, LANES))

    def values_pass(u0):
        rl, j, _, _, cs = unit(u0)
        v_pair = pair_operands(vc_ref, vp_ref, rl, j, cs)
        logits = [(s_ref[(u0 + e) % n_slots], m_ref[(u0 + e) % n_slots]) for e in range(2)]
        p_e, rhs_e = [], []
        for e, (s, m) in enumerate(logits):
            p_e += [jnp.exp2(s[:, :LANES] - m), jnp.exp2(s[:, LANES:] - m)]
            rhs_e.append(jnp.concatenate(
                [jnp.where(head_lanes[e], v_pair, jnp.zeros_like(v_pair)), ones_ref[e]], axis=-1))
        p = jnp.concatenate(p_e, axis=-1).astype(v_pair.dtype)
        acc = jnp.dot(p, jnp.concatenate(rhs_e, axis=0), preferred_element_type=jnp.float32)
        denom = acc[:, LANES:]
        qs = slice(j * QBLOCK, (j + 1) * QBLOCK)
        o_ref[0, rl, qs, cs] = (acc[:, :LANES] / denom).astype(o_ref.dtype)
        lse_ref[0, rl, qs, cs] = (jnp.where(head_lanes[0], logits[0][1], logits[1][1]) + jnp.log2(denom)) * LN2

    n_units = q_ref.shape[1] * blocks_per_sub * n_heads
    n_slots = s_ref.shape[0]
    lag = n_slots // 2
    for u in range(-lag, n_units, 2):
        if u + lag < n_units:
            logits_pass(u + lag)
            logits_pass(u + lag + 1)
        if u >= 0:
            values_pass(u)


def _attention(qkv, bias_tab):
    B, dil, L, _ = qkv.shape
    rows = min(ATTN_STEP_ROWS, L)
    nsub = min(dil, ATTN_STEP_ROWS // rows)
    rpb = rows // QBLOCK
    W = ATTN_WIDTH
    kernel = functools.partial(_attn_kernel, rows=rows)
    ones_cols = jnp.asarray(np.broadcast_to(
        (np.arange(LANES) // HEAD_DIM == np.arange(2)[:, None])[:, None, :], (2, 2 * QBLOCK, LANES)), qkv.dtype)
    cur = lambda col: pl.BlockSpec((1, nsub, rows, W), lambda b, r, n: (b, r, n, col))
    prev = lambda col: pl.BlockSpec((1, nsub, QBLOCK, W),
                                    lambda b, r, n: (b, r, jnp.maximum(n * rpb - 1, 0), col))
    out_spec = pl.BlockSpec((1, nsub, rows, W), lambda b, r, n: (b, r, n, 0))
    return pl.pallas_call(
        kernel,
        grid=(B, dil // nsub, L // rows),
        in_specs=[cur(0), cur(1), prev(1), cur(2), prev(2),
                  pl.BlockSpec((2, HEADS_PER_GROUP, QBLOCK, 2 * QBLOCK), lambda b, r, n: (0, 0, 0, 0)),
                  pl.BlockSpec((2, 2 * QBLOCK, LANES), lambda b, r, n: (0, 0, 0))],
        out_specs=[out_spec, out_spec],
        out_shape=[jax.ShapeDtypeStruct((B, dil, L, W), jnp.bfloat16),
                   jax.ShapeDtypeStruct((B, dil, L, W), jnp.float32)],
        scratch_shapes=[pltpu.VMEM((2 * ATTN_PIPELINE_LAG, QBLOCK, 2 * QBLOCK), jnp.float32),
                        pltpu.VMEM((2 * ATTN_PIPELINE_LAG, QBLOCK, LANES), jnp.float32)],
        compiler_params=pltpu.CompilerParams(vmem_limit_bytes=VMEM_LIMIT_BYTES),
        name=f"band_attn_d{dil}",
    )(qkv, qkv, qkv, qkv, qkv, bias_tab, ones_cols)


def _sigmoid(v):
    return 1.0 / (1.0 + jnp.exp(-v))


OUT_ROW_CHUNK = 128
OUT_MM_ROWS = 256
EXT_HALO = 2 * POOL_HALO


def _out_kernel(x_ref, xprev_ref, g_ref, mod_ref,
                o0_ref, o1_ref, o2_ref, l0_ref, l1_ref, l2_ref,
                w_rest_ref, pool_w_ref, pool_scale_ref, w_attn_ref, w_pool_ref, w_out_ref, fin_ref,
                out_ref,
                h_ref, rest_ref, ext_ref, ptmp_ref, mixed_ref, ga_ref, gp_ref, ya_ref, yp_ref, mg_ref,
                so1_ref, so2_ref, sl1_ref, sl2_ref, *, tm, apply_final_norm):
    i = pl.program_id(1)
    W = ATTN_WIDTH
    bf16 = jnp.bfloat16
    n_slabs = W // LANES
    rows = lambda c: slice(c * OUT_ROW_CHUNK, (c + 1) * OUT_ROW_CHUNK)
    half = lambda a: slice(a * OUT_MM_ROWS, (a + 1) * OUT_MM_ROWS)
    Z_ATTN, U_POOL, Z_POOL, G_ATTN, G_POOL = (0, W), (W, 2 * W), (2 * W, 3 * W), (3 * W, 5 * W), (5 * W, 7 * W)
    silu = lambda z: z * _sigmoid(z)

    def v_pre(c):
        h_ref[rows(c), :] = _prenorm_rows(x_ref[0, rows(c), :], g_ref, mod_ref).astype(bf16)

    def m_proj(a, cols, lo_hi=None):
        lo, hi = lo_hi or cols
        res = jnp.dot(h_ref[half(a), :], w_rest_ref[:, lo:hi], preferred_element_type=jnp.float32)
        if cols == U_POOL:
            ext_ref[EXT_HALO + a * OUT_MM_ROWS:EXT_HALO + (a + 1) * OUT_MM_ROWS, :] = res
        else:
            rest_ref[half(a), lo:hi] = res

    def m_uprev():
        h_prev = _prenorm_rows(xprev_ref[0], g_ref, mod_ref).astype(bf16)
        u_prev = jnp.dot(h_prev, w_rest_ref[:, W:2 * W], preferred_element_type=jnp.float32)
        ext_ref[0:EXT_HALO - POOL_HALO, :] = jnp.zeros((EXT_HALO - POOL_HALO, W), jnp.float32)
        ext_ref[EXT_HALO - POOL_HALO:EXT_HALO, :] = jnp.where(i == 0, 0.0, u_prev)

    def v_tok(ref, stage_ref, r_lo, r_hi):
        d = ref.shape[1]
        for r in range(r_lo, r_hi):
            for cb in range(n_slabs):
                stage_ref[cb, pl.ds(r, tm // d, stride=d), :] = (
                    ref[0, r, :, cb * LANES:(cb + 1) * LANES].astype(jnp.float32))

    def staged(stage_ref, c):
        return jnp.concatenate([stage_ref[cb, rows(c), :] for cb in range(n_slabs)], axis=-1)

    def v_merge(c):
        l0, l1, l2 = l0_ref[0, 0, rows(c), :], staged(sl1_ref, c), staged(sl2_ref, c)
        mx = jnp.maximum(jnp.maximum(l0, l1), l2)
        e0, e1, e2 = jnp.exp(l0 - mx), jnp.exp(l1 - mx), jnp.exp(l2 - mx)
        num = (e0 * o0_ref[0, 0, rows(c), :].astype(jnp.float32) + e1 * staged(so1_ref, c)
               + e2 * staged(so2_ref, c))
        attn = num / (e0 + e1 + e2)
        ga_ref[rows(c), :] = (attn * silu(rest_ref[rows(c), Z_ATTN[0]:Z_ATTN[1]])).astype(bf16)

    def pool(g, a):
        win = POOL_WINDOWS[g]
        cs = slice(g * LANES, (g + 1) * LANES)
        base = EXT_HALO + a * OUT_MM_ROWS
        u = ext_ref[base:base + OUT_MM_ROWS, cs]
        levels = win.bit_length() - 1
        top = EXT_HALO + OUT_MM_ROWS
        s = None
        for lvl in range(1, levels + 1):
            shift, lo = 1 << (lvl - 1), EXT_HALO - 8 * (levels - lvl)
            if lvl == 1:
                r0 = base - EXT_HALO
                s = ext_ref[r0 + lo:r0 + top, cs] + ext_ref[r0 + lo - shift:r0 + top - shift, cs]
            else:
                s = ptmp_ref[g, a, lvl - 2, lo:top, :] + ptmp_ref[g, a, lvl - 2, lo - shift:top - shift, :]
            if lvl < levels:
                ptmp_ref[g, a, lvl - 1, lo:top, :] = s
        t = i * tm + a * OUT_MM_ROWS + lax.broadcasted_iota(jnp.int32, (OUT_MM_ROWS, 1), 0)
        cnt = jnp.minimum(t + 1, win).astype(jnp.float32)
        pooled = s / cnt - u
        mixed_ref[half(a), cs] = jnp.dot(pooled.astype(bf16), pool_w_ref[g],
                                         preferred_element_type=jnp.float32) * pool_scale_ref[:, cs]

    def v_gate_p(c):
        gp_ref[rows(c), :] = (mixed_ref[rows(c), :] * silu(rest_ref[rows(c), Z_POOL[0]:Z_POOL[1]])).astype(bf16)

    def m_yp(a):
        yp_ref[half(a), :] = jnp.dot(gp_ref[half(a), :], w_pool_ref[...], preferred_element_type=jnp.float32)

    def m_ya(a):
        ya_ref[half(a), :] = jnp.dot(ga_ref[half(a), :], w_attn_ref[...], preferred_element_type=jnp.float32)

    def v_sig(c):
        mg_ref[rows(c), :] = (_sigmoid(rest_ref[rows(c), G_ATTN[0]:G_ATTN[1]]) * ya_ref[rows(c), :]
                              + _sigmoid(rest_ref[rows(c), G_POOL[0]:G_POOL[1]]) * yp_ref[rows(c), :]).astype(bf16)

    def m_out(a):
        out_ref[0, half(a), :] = jnp.dot(mg_ref[half(a), :], w_out_ref[...], preferred_element_type=jnp.float32)

    def v_fin(c):
        y = x_ref[0, rows(c), :] + mod_ref[0, 2:3, :] * out_ref[0, rows(c), :]
        if apply_final_norm:
            y = y * lax.rsqrt(jnp.mean(y * y, axis=-1, keepdims=True) + EPS) * fin_ref[...]
        out_ref[0, rows(c), :] = y

    d1, d2 = l1_ref.shape[1], l2_ref.shape[1]
    q2 = d2 // 4
    g_halves = lambda cols: [(cols[0], cols[0] + W), (cols[0] + W, cols[1])]
    v_pre(0); v_pre(1); m_uprev()
    m_proj(0, U_POOL); v_pre(2)
    m_proj(0, Z_POOL); v_pre(3)
    m_proj(1, U_POOL); v_tok(l1_ref, sl1_ref, 0, d1)
    m_proj(1, Z_POOL); v_tok(o1_ref, so1_ref, 0, d1)
    m_proj(0, Z_ATTN); pool(0, 0); pool(0, 1); pool(1, 0)
    m_proj(1, Z_ATTN); pool(1, 1); pool(2, 0)
    m_proj(0, G_POOL, g_halves(G_POOL)[0]); pool(2, 1)
    m_proj(0, G_POOL, g_halves(G_POOL)[1]); pool(3, 0)
    m_proj(1, G_POOL, g_halves(G_POOL)[0]); pool(3, 1)
    m_proj(1, G_POOL, g_halves(G_POOL)[1]); v_tok(l2_ref, sl2_ref, 0, q2); v_tok(l2_ref, sl2_ref, q2, 2 * q2)
    m_proj(0, G_ATTN, g_halves(G_ATTN)[0]); v_tok(l2_ref, sl2_ref, 2 * q2, 3 * q2); v_tok(l2_ref, sl2_ref, 3 * q2, d2)
    m_proj(0, G_ATTN, g_halves(G_ATTN)[1]); v_tok(o2_ref, so2_ref, 0, 2 * q2)
    m_proj(1, G_ATTN, g_halves(G_ATTN)[0]); v_tok(o2_ref, so2_ref, 2 * q2, d2)
    m_proj(1, G_ATTN, g_halves(G_ATTN)[1]); v_gate_p(0); v_gate_p(1); v_merge(0)
    m_yp(0); v_gate_p(2); v_gate_p(3); v_merge(1)
    m_yp(1); v_merge(2); v_merge(3)
    m_ya(0); v_sig(0)
    m_ya(1); v_sig(1)
    v_sig(2); v_sig(3)
    m_out(0); v_fin(0)
    m_out(1); v_fin(1)
    v_fin(2); v_fin(3)


def _merge_out(x, norm_g, mod3, attn_parts, w_rest, pool_w, pool_scale, w_attn_br, w_pool_br, w_out,
               final_g, apply_final_norm):
    B, S, D = x.shape
    tm = 512
    W = ATTN_WIDTH
    hb = tm // POOL_HALO
    tok = lambda width: pl.BlockSpec((1, tm, width), lambda b, i: (b, i, 0))
    const = lambda shape: pl.BlockSpec(shape, lambda b, i: (0,) * len(shape), pipeline_mode=pl.Buffered(1))
    (o0, l0), (o1, l1), (o2, l2) = attn_parts
    sub = lambda a: pl.BlockSpec((1, a.shape[1], tm // a.shape[1], W), lambda b, i: (b, 0, i, 0))
    stage = pltpu.VMEM((W // LANES, tm, LANES), jnp.float32)
    kernel = functools.partial(_out_kernel, tm=tm, apply_final_norm=apply_final_norm)
    return pl.pallas_call(
        kernel,
        grid=(B, S // tm),
        in_specs=[tok(D),
                  pl.BlockSpec((1, POOL_HALO, D), lambda b, i: (b, jnp.maximum(i * hb - 1, 0), 0)),
                  const((1, D)),
                  pl.BlockSpec((1, 3, D), lambda b, i: (b, 0, 0)),
                  sub(o0), sub(o1), sub(o2), sub(l0), sub(l1), sub(l2),
                  const(w_rest.shape), const(pool_w.shape), const((1, POOL_GROUPS * LANES)),
                  const(w_attn_br.shape), const(w_pool_br.shape), const(w_out.shape), const((1, D))],
        out_specs=tok(D),
        out_shape=jax.ShapeDtypeStruct((B, S, D), x.dtype),
        scratch_shapes=[pltpu.VMEM((tm, D), jnp.bfloat16),
                        pltpu.VMEM((tm, w_rest.shape[1]), jnp.float32),
                        pltpu.VMEM((tm + EXT_HALO, POOL_GROUPS * LANES), jnp.float32),
                        pltpu.VMEM((POOL_GROUPS, tm // OUT_MM_ROWS, 3, EXT_HALO + OUT_MM_ROWS, LANES),
                                   jnp.float32),
                        pltpu.VMEM((tm, W), jnp.float32),
                        pltpu.VMEM((tm, W), jnp.bfloat16),
                        pltpu.VMEM((tm, W), jnp.bfloat16),
                        pltpu.VMEM((tm, D), jnp.float32),
                        pltpu.VMEM((tm, D), jnp.float32),
                        pltpu.VMEM((tm, D), jnp.bfloat16),
                        stage, stage, stage, stage],
        compiler_params=pltpu.CompilerParams(vmem_limit_bytes=VMEM_LIMIT_BYTES),
        name="merge_out",
    )(x, x, norm_g.reshape(1, D), mod3, o0, o1, o2, l0, l1, l2,
      w_rest, pool_w, pool_scale.reshape(1, -1), w_attn_br, w_pool_br, w_out, final_g.reshape(1, D))


def kernel(x, c, norm_g, w_ada, b_ada, w_in, pool_w, pool_scale, w_attn_br, w_pool_br, w_out, rel_bias, final_g):
    B, S, D = x.shape
    depth = norm_g.shape[0]
    bf16 = jnp.bfloat16
    col = np.arange(QKV_WIDTH) % QKV_GROUP_WIDTH
    q_scale = jnp.asarray(np.where(col < ATTN_WIDTH, HEAD_DIM ** -0.5 * LOG2E, 1.0), jnp.float32)
    tables = []
    for gi, (win, dil) in enumerate(ATTN_GROUPS):
        assert win // dil <= QBLOCK and S % (dil * QBLOCK) == 0
        tables.append(_bias_tables(rel_bias[:, gi * HEADS_PER_GROUP:(gi + 1) * HEADS_PER_GROUP], dil, win // dil))

    for l in range(depth):
        w_qkv = (w_in[l, :, :QKV_WIDTH] * q_scale).astype(bf16)
        w_rest = w_in[l, :, QKV_WIDTH:].astype(bf16)
        mod3 = _modulation(c, w_ada[l], b_ada[l]).reshape(B, 3, D)
        attn_parts = []
        for gi, (win, dil) in enumerate(ATTN_GROUPS):
            qkv = _qkv_proj(x, norm_g[l], mod3, w_qkv[:, gi * QKV_GROUP_WIDTH:(gi + 1) * QKV_GROUP_WIDTH], dil)
            attn_parts.append(_attention(qkv, tables[gi]))
        x = _merge_out(x, norm_g[l], mod3, attn_parts, w_rest, pool_w[l].astype(bf16), pool_scale[l],
                       w_attn_br[l].astype(bf16), w_pool_br[l].astype(bf16), w_out[l].astype(bf16),
                       final_g, apply_final_norm=(l == depth - 1))
    return x
```

```python
import functools
import math

import jax
import jax.numpy as jnp
import numpy as np
from jax import lax
from jax.experimental import pallas as pl
from jax.experimental.pallas import tpu as pltpu

HEAD_DIM = 64
HEADS_PER_GROUP = 8
ATTN_GROUPS = ((128, 1), (512, 4), (2048, 16))
N_GROUPS = len(ATTN_GROUPS)
ATTN_WIDTH = HEADS_PER_GROUP * HEAD_DIM
QKV_GROUP_WIDTH = 3 * ATTN_WIDTH
QKV_WIDTH = N_GROUPS * QKV_GROUP_WIDTH
QBLOCK = 128
ATTN_STEP_ROWS = 2048
ATTN_PIPELINE_LAG = 4
POOL_WINDOWS = (2, 4, 8, 16)
POOL_GROUPS = len(POOL_WINDOWS)
POOL_HALO = 16
NUM_BUCKETS = 32
MAX_DISTANCE = 2048
EPS = 1e-6
MASKED = -1e30
LOG2E = math.log2(math.e)
LN2 = math.log(2.0)
LANES = 128
VMEM_LIMIT_BYTES = 56 * 1024 * 1024


def _t5_bucket(n):
    max_exact = NUM_BUCKETS // 2
    nf = np.maximum(n, 1).astype(np.float32)
    large = max_exact + (np.log(nf / np.float32(max_exact)) / np.float32(math.log(MAX_DISTANCE / max_exact))
                         * np.float32(NUM_BUCKETS - max_exact)).astype(np.int32)
    large = np.minimum(large, NUM_BUCKETS - 1)
    return np.where(n < max_exact, n, large).astype(np.int32)


def _bias_tables(bias_g, dil, n_back):
    i = np.arange(QBLOCK)[:, None]
    j = np.arange(2 * QBLOCK)[None, :]
    dist = QBLOCK + i - j
    ok = (dist >= 0) & (dist <= n_back)
    bucket = jnp.asarray(_t5_bucket(np.clip(dist, 0, n_back) * dil))
    onehot = (bucket[None] == jnp.arange(NUM_BUCKETS, dtype=jnp.int32)[:, None, None]).astype(jnp.float32)
    bias = jnp.einsum('bh,bqk->hqk', bias_g.astype(jnp.float32), onehot,
                      precision=lax.Precision.HIGHEST)
    bias = bias * LOG2E
    full = jnp.where(ok[None], bias, MASKED)
    first = jnp.where((ok & (j >= QBLOCK))[None], bias, MASKED)
    return jnp.stack([full, first], axis=0)


def _mod_kernel(c_ref, w_ref, b_ref, o_ref):
    o_ref[...] = jnp.dot(c_ref[...], w_ref[...], preferred_element_type=jnp.float32,
                         precision=lax.Precision.HIGHEST) + b_ref[...]


def _modulation(c, w_ada, b_ada):
    B, D = c.shape
    N = w_ada.shape[1]
    tn = 1024
    return pl.pallas_call(
        _mod_kernel,
        grid=(N // tn,),
        in_specs=[pl.BlockSpec((B, D), lambda j: (0, 0)),
                  pl.BlockSpec((D, tn), lambda j: (0, j)),
                  pl.BlockSpec((1, tn), lambda j: (0, j))],
        out_specs=pl.BlockSpec((B, tn), lambda j: (0, j)),
        out_shape=jax.ShapeDtypeStruct((B, N), jnp.float32),
        name="adaln_mod",
    )(c, w_ada, b_ada.reshape(1, N))


def _prenorm_rows(x, g_ref, mod_ref):
    y = x * lax.rsqrt(jnp.mean(x * x, axis=-1, keepdims=True) + EPS)
    y = y * g_ref[...]
    return y * (1.0 + mod_ref[0, 1:2, :]) + mod_ref[0, 0:1, :]


def _qkv_kernel(x_ref, g_ref, mod_ref, w_ref, *refs, dils, tt, tc):
    o_refs, stage_ref = refs[:len(dils)], refs[len(dils)]
    n_slabs = x_ref.shape[-1] // LANES
    gw = QKV_GROUP_WIDTH

    def regroup(c):
        h = _prenorm_rows(x_ref[0, c * tc:(c + 1) * tc, :], g_ref, mod_ref)
        for cb in range(n_slabs):
            stage_ref[c % 2, cb] = h[:, cb * LANES:(cb + 1) * LANES]
        hs = []
        for d in dils:
            if d == 1:
                hs.append(h.astype(w_ref.dtype))
            else:
                hs.append(jnp.concatenate(
                    [jnp.concatenate([stage_ref[c % 2, cb, pl.ds(r, tc // d, stride=d), :]
                                      for cb in range(n_slabs)], axis=-1) for r in range(d)],
                    axis=0).astype(w_ref.dtype))
        return hs

    def project(c, hs):
        for gi, (d, h, o_ref) in enumerate(zip(dils, hs, o_refs)):
            res = jnp.dot(h, w_ref[:, gi * gw:(gi + 1) * gw], preferred_element_type=jnp.float32).astype(o_ref.dtype)
            rows = tc // d
            for r in range(d):
                o_ref[0, r, c * rows:(c + 1) * rows, :] = res[r * rows:(r + 1) * rows]

    h_next = regroup(0)
    for c in range(tt // tc):
        h_cur = h_next
        if (c + 1) * tc < tt:
            h_next = regroup(c + 1)
        project(c, h_cur)


def _qkv_proj(x, norm_g, mod3, w_qkv, dils):
    B, S, D = x.shape
    tt = 1024
    tc = 256
    kernel = functools.partial(_qkv_kernel, dils=dils, tt=tt, tc=tc)
    single = lambda shape: pl.BlockSpec(shape, lambda b, i: (0,) * len(shape), pipeline_mode=pl.Buffered(1))
    return pl.pallas_call(
        kernel,
        grid=(B, S // tt),
        in_specs=[pl.BlockSpec((1, tt, D), lambda b, i: (b, i, 0)),
                  single((1, D)),
                  pl.BlockSpec((1, 3, D), lambda b, i: (b, 0, 0)),
                  single(w_qkv.shape)],
        out_specs=[pl.BlockSpec((1, d, tt // d, QKV_GROUP_WIDTH), lambda b, i: (b, 0, i, 0)) for d in dils],
        out_shape=[jax.ShapeDtypeStruct((B, d, S // d, QKV_GROUP_WIDTH), jnp.bfloat16) for d in dils],
        scratch_shapes=[pltpu.VMEM((2, D // LANES, tc, LANES), jnp.float32)],
        compiler_params=pltpu.CompilerParams(vmem_limit_bytes=VMEM_LIMIT_BYTES),
        name="qkv_proj",
    )(x, norm_g.reshape(1, D), mod3, w_qkv)


def _attn_kernel(q_ref, kc_ref, kp_ref, vc_ref, vp_ref, bias_ref, o_ref, lse_ref, s_ref, m_ref, *, rows):
    n = pl.program_id(2)
    first_sel = jnp.where(n == 0, 1, 0)
    lane = lax.broadcasted_iota(jnp.int32, (1, LANES), 1)
    head_lanes = [lane < HEAD_DIM, lane >= HEAD_DIM]
    rhs_head = lax.broadcasted_iota(jnp.int32, (4 * QBLOCK, 2 * LANES), 0) // (2 * QBLOCK)
    rhs_lane_head = (lax.broadcasted_iota(jnp.int32, (4 * QBLOCK, 2 * LANES), 1) % LANES) // HEAD_DIM
    rhs_keep = rhs_head == rhs_lane_head
    n_heads = HEADS_PER_GROUP

    blocks_per_sub = rows // QBLOCK

    def pair_operands(ref_c, ref_p, rl, j, cs):
        if j == 0:
            return jnp.concatenate([ref_p[0, rl, :, cs], ref_c[0, rl, 0:QBLOCK, cs]], axis=0)
        return ref_c[0, rl, (j - 1) * QBLOCK:(j + 1) * QBLOCK, cs]

    def unit(u):
        blk, h = divmod(u, n_heads)
        rl, j = divmod(blk, blocks_per_sub)
        hp, e = divmod(h, 2)
        return rl, j, h, e, slice(hp * LANES, (hp + 1) * LANES)

    def logits_pass(u):
        rl, j, h, e, cs = unit(u)
        q_pair = q_ref[0, rl, j * QBLOCK:(j + 1) * QBLOCK, cs]
        k_pair = pair_operands(kc_ref, kp_ref, rl, j, cs)
        q_h = jnp.where(head_lanes[e], q_pair, jnp.zeros_like(q_pair))
        s = lax.dot_general(q_h, k_pair, (((1,), (1,)), ((), ())), preferred_element_type=jnp.float32)
        s = s + (bias_ref[first_sel, h] if j == 0 else bias_ref[0, h])
        s_ref[u % n_slots] = s
        m_ref[u % n_slots] = jnp.broadcast_to(jnp.max(s, axis=-1, keepdims=True), (QBLOCK, LANES))

    def values_pass(u0):
        rl, j, _, _, cs = unit(u0)
        v_pair = pair_operands(vc_ref, vp_ref, rl, j, cs)
        logits = [(s_ref[(u0 + e) % n_slots], m_ref[(u0 + e) % n_slots]) for e in range(2)]
        p_e = []
        for s, m in logits:
            p_e += [jnp.exp2(s[:, :LANES] - m), jnp.exp2(s[:, LANES:] - m)]
        p = jnp.concatenate(p_e, axis=-1).astype(v_pair.dtype)
        v_ones = jnp.concatenate([v_pair, jnp.ones_like(v_pair)], axis=-1)
        rhs = jnp.where(rhs_keep, jnp.concatenate([v_ones, v_ones], axis=0), jnp.zeros((), v_pair.dtype))
        acc = jnp.dot(p, rhs, preferred_element_type=jnp.float32)
        denom = acc[:, LANES:]
        qs = slice(j * QBLOCK, (j + 1) * QBLOCK)
        o_ref[0, rl, qs, cs] = (acc[:, :LANES] / denom).astype(o_ref.dtype)
        lse_ref[0, rl, qs, cs] = jnp.where(head_lanes[0], logits[0][1], logits[1][1]) * LN2 + jnp.log(denom)

    n_units = q_ref.shape[1] * blocks_per_sub * n_heads
    n_slots = s_ref.shape[0]
    lag = n_slots // 2
    for u in range(-lag, n_units, 2):
        if u + lag < n_units:
            logits_pass(u + lag)
            logits_pass(u + lag + 1)
        if u >= 0:
            values_pass(u)


def _attention(qkv, bias_tab):
    B, dil, L, _ = qkv.shape
    rows = min(ATTN_STEP_ROWS, L)
    nsub = min(dil, ATTN_STEP_ROWS // rows)
    rpb = rows // QBLOCK
    W = ATTN_WIDTH
    kernel = functools.partial(_attn_kernel, rows=rows)
    cur = lambda col: pl.BlockSpec((1, nsub, rows, W), lambda b, r, n: (b, r, n, col))
    prev = lambda col: pl.BlockSpec((1, nsub, QBLOCK, W),
                                    lambda b, r, n: (b, r, jnp.maximum(n * rpb - 1, 0), col))
    out_spec = pl.BlockSpec((1, nsub, rows, W), lambda b, r, n: (b, r, n, 0))
    return pl.pallas_call(
        kernel,
        grid=(B, dil // nsub, L // rows),
        in_specs=[cur(0), cur(1), prev(1), cur(2), prev(2),
                  pl.BlockSpec((2, HEADS_PER_GROUP, QBLOCK, 2 * QBLOCK), lambda b, r, n: (0, 0, 0, 0),
                               pipeline_mode=pl.Buffered(1))],
        out_specs=[out_spec, out_spec],
        out_shape=[jax.ShapeDtypeStruct((B, dil, L, W), jnp.bfloat16),
                   jax.ShapeDtypeStruct((B, dil, L, W), jnp.float32)],
        scratch_shapes=[pltpu.VMEM((2 * ATTN_PIPELINE_LAG, QBLOCK, 2 * QBLOCK), jnp.float32),
                        pltpu.VMEM((2 * ATTN_PIPELINE_LAG, QBLOCK, LANES), jnp.float32)],
        compiler_params=pltpu.CompilerParams(vmem_limit_bytes=VMEM_LIMIT_BYTES),
        name=f"band_attn_d{dil}",
    )(qkv, qkv, qkv, qkv, qkv, bias_tab)


def _sigmoid(v):
    return 1.0 / (1.0 + jnp.exp(-v))


OUT_ROW_CHUNK = 128
OUT_MM_ROWS = 256
EXT_HALO = 2 * POOL_HALO


def _out_kernel(x_ref, xprev_ref, g_ref, mod_ref,
                o0_ref, o1_ref, o2_ref, l0_ref, l1_ref, l2_ref,
                w_rest_ref, pool_w_ref, pool_scale_ref, w_attn_ref, w_pool_ref, w_out_ref, fin_ref,
                out_ref,
                h_ref, rest_ref, ext_ref, ptmp_ref, mixed_ref, ga_ref, gp_ref, ya_ref, yp_ref, mg_ref,
                so1_ref, so2_ref, sl1_ref, sl2_ref, *, tm, apply_final_norm):
    i = pl.program_id(1)
    W = ATTN_WIDTH
    bf16 = jnp.bfloat16
    n_slabs = W // LANES
    rows = lambda c: slice(c * OUT_ROW_CHUNK, (c + 1) * OUT_ROW_CHUNK)
    half = lambda a: slice(a * OUT_MM_ROWS, (a + 1) * OUT_MM_ROWS)
    Z_ATTN, U_POOL, Z_POOL, G_ATTN, G_POOL = (0, W), (W, 2 * W), (2 * W, 3 * W), (3 * W, 5 * W), (5 * W, 7 * W)
    silu = lambda z: z * _sigmoid(z)

    def v_pre(c):
        h_ref[rows(c), :] = _prenorm_rows(x_ref[0, rows(c), :], g_ref, mod_ref).astype(bf16)

    def m_proj(a, cols, lo_hi=None):
        lo, hi = lo_hi or cols
        res = jnp.dot(h_ref[half(a), :], w_rest_ref[:, lo:hi], preferred_element_type=jnp.float32)
        if cols == U_POOL:
            ext_ref[EXT_HALO + a * OUT_MM_ROWS:EXT_HALO + (a + 1) * OUT_MM_ROWS, :] = res
        else:
            rest_ref[half(a), lo:hi] = res

    def m_uprev():
        h_prev = _prenorm_rows(xprev_ref[0], g_ref, mod_ref).astype(bf16)
        u_prev = jnp.dot(h_prev, w_rest_ref[:, W:2 * W], preferred_element_type=jnp.float32)
        ext_ref[0:EXT_HALO - POOL_HALO, :] = jnp.zeros((EXT_HALO - POOL_HALO, W), jnp.float32)
        ext_ref[EXT_HALO - POOL_HALO:EXT_HALO, :] = jnp.where(i == 0, 0.0, u_prev)

    def v_tok(ref, stage_ref, r_lo, r_hi):
        d = ref.shape[1]
        for r in range(r_lo, r_hi):
            for cb in range(n_slabs):
                stage_ref[cb, pl.ds(r, tm // d, stride=d), :] = (
                    ref[0, r, :, cb * LANES:(cb + 1) * LANES].astype(jnp.float32))

    def staged(stage_ref, c):
        return jnp.concatenate([stage_ref[cb, rows(c), :] for cb in range(n_slabs)], axis=-1)

    def v_merge(c):
        l0, l1, l2 = l0_ref[0, 0, rows(c), :], staged(sl1_ref, c), staged(sl2_ref, c)
        mx = jnp.maximum(jnp.maximum(l0, l1), l2)
        e0, e1, e2 = jnp.exp(l0 - mx), jnp.exp(l1 - mx), jnp.exp(l2 - mx)
        num = (e0 * o0_ref[0, 0, rows(c), :].astype(jnp.float32) + e1 * staged(so1_ref, c)
               + e2 * staged(so2_ref, c))
        attn = num / (e0 + e1 + e2)
        ga_ref[rows(c), :] = (attn * silu(rest_ref[rows(c), Z_ATTN[0]:Z_ATTN[1]])).astype(bf16)

    def pool(g, a):
        win = POOL_WINDOWS[g]
        cs = slice(g * LANES, (g + 1) * LANES)
        base = EXT_HALO + a * OUT_MM_ROWS
        u = ext_ref[base:base + OUT_MM_ROWS, cs]
        levels = win.bit_length() - 1
        top = EXT_HALO + OUT_MM_ROWS
        s = None
        for lvl in range(1, levels + 1):
            shift, lo = 1 << (lvl - 1), EXT_HALO - 8 * (levels - lvl)
            if lvl == 1:
                r0 = base - EXT_HALO
                s = ext_ref[r0 + lo:r0 + top, cs] + ext_ref[r0 + lo - shift:r0 + top - shift, cs]
            else:
                s = ptmp_ref[g, a, lvl - 2, lo:top, :] + ptmp_ref[g, a, lvl - 2, lo - shift:top - shift, :]
            if lvl < levels:
                ptmp_ref[g, a, lvl - 1, lo:top, :] = s
        t = i * tm + a * OUT_MM_ROWS + lax.broadcasted_iota(jnp.int32, (OUT_MM_ROWS, 1), 0)
        cnt = jnp.minimum(t + 1, win).astype(jnp.float32)
        pooled = s / cnt - u
        mixed_ref[half(a), cs] = jnp.dot(pooled.astype(bf16), pool_w_ref[g],
                                         preferred_element_type=jnp.float32) * pool_scale_ref[:, cs]

    def v_gate_p(c):
        gp_ref[rows(c), :] = (mixed_ref[rows(c), :] * silu(rest_ref[rows(c), Z_POOL[0]:Z_POOL[1]])).astype(bf16)

    def m_yp(a):
        yp_ref[half(a), :] = jnp.dot(gp_ref[half(a), :], w_pool_ref[...], preferred_element_type=jnp.float32)

    def m_ya(a):
        ya_ref[half(a), :] = jnp.dot(ga_ref[half(a), :], w_attn_ref[...], preferred_element_type=jnp.float32)

    def v_sig(c):
        mg_ref[rows(c), :] = (_sigmoid(rest_ref[rows(c), G_ATTN[0]:G_ATTN[1]]) * ya_ref[rows(c), :]
                              + _sigmoid(rest_ref[rows(c), G_POOL[0]:G_POOL[1]]) * yp_ref[rows(c), :]).astype(bf16)

    def m_out(a):
        out_ref[0, half(a), :] = jnp.dot(mg_ref[half(a), :], w_out_ref[...], preferred_element_type=jnp.float32)

    def v_fin(c):
        y = x_ref[0, rows(c), :] + mod_ref[0, 2:3, :] * out_ref[0, rows(c), :]
        if apply_final_norm:
            y = y * lax.rsqrt(jnp.mean(y * y, axis=-1, keepdims=True) + EPS) * fin_ref[...]
        out_ref[0, rows(c), :] = y

    d1, d2 = l1_ref.shape[1], l2_ref.shape[1]
    q2 = d2 // 4
    g_halves = lambda cols: [(cols[0], cols[0] + W), (cols[0] + W, cols[1])]
    v_pre(0); v_pre(1); m_uprev()
    m_proj(0, U_POOL); v_pre(2)
    m_proj(0, Z_POOL); v_pre(3)
    m_proj(1, U_POOL); v_tok(l1_ref, sl1_ref, 0, d1)
    m_proj(1, Z_POOL); v_tok(o1_ref, so1_ref, 0, d1)
    m_proj(0, Z_ATTN); pool(0, 0); pool(0, 1); pool(1, 0)
    m_proj(1, Z_ATTN); pool(1, 1); pool(2, 0)
    m_proj(0, G_POOL, g_halves(G_POOL)[0]); pool(2, 1)
    m_proj(0, G_POOL, g_halves(G_POOL)[1]); pool(3, 0)
    m_proj(1, G_POOL, g_halves(G_POOL)[0]); pool(3, 1)
    m_proj(1, G_POOL, g_halves(G_POOL)[1]); v_tok(l2_ref, sl2_ref, 0, q2); v_tok(l2_ref, sl2_ref, q2, 2 * q2)
    m_proj(0, G_ATTN, g_halves(G_ATTN)[0]); v_tok(l2_ref, sl2_ref, 2 * q2, 3 * q2); v_tok(l2_ref, sl2_ref, 3 * q2, d2)
    m_proj(0, G_ATTN, g_halves(G_ATTN)[1]); v_tok(o2_ref, so2_ref, 0, 2 * q2)
    m_proj(1, G_ATTN, g_halves(G_ATTN)[0]); v_tok(o2_ref, so2_ref, 2 * q2, d2)
    m_proj(1, G_ATTN, g_halves(G_ATTN)[1]); v_gate_p(0); v_gate_p(1); v_merge(0)
    m_yp(0); v_gate_p(2); v_gate_p(3); v_merge(1)
    m_yp(1); v_merge(2); v_merge(3)
    m_ya(0); v_sig(0)
    m_ya(1); v_sig(1)
    v_sig(2); v_sig(3)
    m_out(0); v_fin(0)
    m_out(1); v_fin(1)
    v_fin(2); v_fin(3)


def _merge_out(x, norm_g, mod3, attn_parts, w_rest, pool_w, pool_scale, w_attn_br, w_pool_br, w_out,
               final_g, apply_final_norm):
    B, S, D = x.shape
    tm = 512
    W = ATTN_WIDTH
    hb = tm // POOL_HALO
    tok = lambda width: pl.BlockSpec((1, tm, width), lambda b, i: (b, i, 0))
    const = lambda shape: pl.BlockSpec(shape, lambda b, i: (0,) * len(shape), pipeline_mode=pl.Buffered(1))
    (o0, l0), (o1, l1), (o2, l2) = attn_parts
    sub = lambda a: pl.BlockSpec((1, a.shape[1], tm // a.shape[1], W), lambda b, i: (b, 0, i, 0))
    stage = pltpu.VMEM((W // LANES, tm, LANES), jnp.float32)
    kernel = functools.partial(_out_kernel, tm=tm, apply_final_norm=apply_final_norm)
    return pl.pallas_call(
        kernel,
        grid=(B, S // tm),
        in_specs=[tok(D),
                  pl.BlockSpec((1, POOL_HALO, D), lambda b, i: (b, jnp.maximum(i * hb - 1, 0), 0)),
                  const((1, D)),
                  pl.BlockSpec((1, 3, D), lambda b, i: (b, 0, 0)),
                  sub(o0), sub(o1), sub(o2), sub(l0), sub(l1), sub(l2),
                  const(w_rest.shape), const(pool_w.shape), const((1, POOL_GROUPS * LANES)),
                  const(w_attn_br.shape), const(w_pool_br.shape), const(w_out.shape), const((1, D))],
        out_specs=tok(D),
        out_shape=jax.ShapeDtypeStruct((B, S, D), x.dtype),
        scratch_shapes=[pltpu.VMEM((tm, D), jnp.bfloat16),
                        pltpu.VMEM((tm, w_rest.shape[1]), jnp.float32),
                        pltpu.VMEM((tm + EXT_HALO, POOL_GROUPS * LANES), jnp.float32),
                        pltpu.VMEM((POOL_GROUPS, tm // OUT_MM_ROWS, 3, EXT_HALO + OUT_MM_ROWS, LANES),
                                   jnp.float32),
                        pltpu.VMEM((tm, W), jnp.float32),
                        pltpu.VMEM((tm, W), jnp.bfloat16),
                        pltpu.VMEM((tm, W), jnp.bfloat16),
                        pltpu.VMEM((tm, D), jnp.float32),
                        pltpu.VMEM((tm, D), jnp.float32),
                        pltpu.VMEM((tm, D), jnp.bfloat16),
                        stage, stage, stage, stage],
        compiler_params=pltpu.CompilerParams(vmem_limit_bytes=VMEM_LIMIT_BYTES),
        name="merge_out",
    )(x, x, norm_g.reshape(1, D), mod3, o0, o1, o2, l0, l1, l2,
      w_rest, pool_w, pool_scale.reshape(1, -1), w_attn_br, w_pool_br, w_out, final_g.reshape(1, D))


def kernel(x, c, norm_g, w_ada, b_ada, w_in, pool_w, pool_scale, w_attn_br, w_pool_br, w_out, rel_bias, final_g):
    B, S, D = x.shape
    depth = norm_g.shape[0]
    bf16 = jnp.bfloat16
    col = np.arange(QKV_WIDTH) % QKV_GROUP_WIDTH
    q_scale = jnp.asarray(np.where(col < ATTN_WIDTH, HEAD_DIM ** -0.5 * LOG2E, 1.0), jnp.float32)
    tables = []
    for gi, (win, dil) in enumerate(ATTN_GROUPS):
        assert win // dil <= QBLOCK and S % (dil * QBLOCK) == 0
        tables.append(_bias_tables(rel_bias[:, gi * HEADS_PER_GROUP:(gi + 1) * HEADS_PER_GROUP], dil, win // dil))

    for l in range(depth):
        w_qkv = (w_in[l, :, :QKV_WIDTH] * q_scale).astype(bf16)
        w_rest = w_in[l, :, QKV_WIDTH:].astype(bf16)
        mod3 = _modulation(c, w_ada[l], b_ada[l]).reshape(B, 3, D)
        qkvs = _qkv_proj(x, norm_g[l], mod3, w_qkv, tuple(dil for _, dil in ATTN_GROUPS))
        attn_parts = [_attention(qkv, table) for qkv, table in zip(qkvs, tables)]
        x = _merge_out(x, norm_g[l], mod3, attn_parts, w_rest, pool_w[l].astype(bf16), pool_scale[l],
                       w_attn_br[l].astype(bf16), w_pool_br[l].astype(bf16), w_out[l].astype(bf16),
                       final_g, apply_final_norm=(l == depth - 1))
    return x
```

```python
import functools
import math

import jax
import jax.numpy as jnp
import numpy as np
from jax import lax
from jax.experimental import pallas as pl
from jax.experimental.pallas import tpu as pltpu

HEAD_DIM = 64
HEADS_PER_GROUP = 8
ATTN_GROUPS = ((128, 1), (512, 4), (2048, 16))
N_GROUPS = len(ATTN_GROUPS)
ATTN_WIDTH = HEADS_PER_GROUP * HEAD_DIM
QKV_GROUP_WIDTH = 3 * ATTN_WIDTH
QKV_WIDTH = N_GROUPS * QKV_GROUP_WIDTH
QBLOCK = 128
ATTN_STEP_ROWS = 2048
ATTN_PIPELINE_LAG = 6
POOL_WINDOWS = (2, 4, 8, 16)
POOL_GROUPS = len(POOL_WINDOWS)
POOL_HALO = 16
NUM_BUCKETS = 32
MAX_DISTANCE = 2048
EPS = 1e-6
MASKED = -1e30
LOG2E = math.log2(math.e)
LN2 = math.log(2.0)
LANES = 128
VMEM_LIMIT_BYTES = 56 * 1024 * 1024


def _t5_bucket(n):
    max_exact = NUM_BUCKETS // 2
    nf = np.maximum(n, 1).astype(np.float32)
    large = max_exact + (np.log(nf / np.float32(max_exact)) / np.float32(math.log(MAX_DISTANCE / max_exact))
                         * np.float32(NUM_BUCKETS - max_exact)).astype(np.int32)
    large = np.minimum(large, NUM_BUCKETS - 1)
    return np.where(n < max_exact, n, large).astype(np.int32)


def _bias_tables(bias_g, dil, n_back):
    i = np.arange(QBLOCK)[:, None]
    j = np.arange(2 * QBLOCK)[None, :]
    dist = QBLOCK + i - j
    ok = (dist >= 0) & (dist <= n_back)
    bucket = jnp.asarray(_t5_bucket(np.clip(dist, 0, n_back) * dil))
    onehot = (bucket[None] == jnp.arange(NUM_BUCKETS, dtype=jnp.int32)[:, None, None]).astype(jnp.float32)
    bias = jnp.einsum('bh,bqk->hqk', bias_g.astype(jnp.float32), onehot,
                      precision=lax.Precision.HIGHEST)
    bias = bias * LOG2E
    full = jnp.where(ok[None], bias, MASKED)
    first = jnp.where((ok & (j >= QBLOCK))[None], bias, MASKED)
    return jnp.stack([full, first], axis=0)


def _mod_kernel(c_ref, w_ref, b_ref, o_ref):
    o_ref[...] = jnp.dot(c_ref[...], w_ref[...], preferred_element_type=jnp.float32,
                         precision=lax.Precision.HIGHEST) + b_ref[...]


def _modulation(c, w_ada, b_ada):
    B, D = c.shape
    N = w_ada.shape[1]
    tn = 1024
    return pl.pallas_call(
        _mod_kernel,
        grid=(N // tn,),
        in_specs=[pl.BlockSpec((B, D), lambda j: (0, 0)),
                  pl.BlockSpec((D, tn), lambda j: (0, j)),
                  pl.BlockSpec((1, tn), lambda j: (0, j))],
        out_specs=pl.BlockSpec((B, tn), lambda j: (0, j)),
        out_shape=jax.ShapeDtypeStruct((B, N), jnp.float32),
        name="adaln_mod",
    )(c, w_ada, b_ada.reshape(1, N))


def _prenorm_rows(x, g_ref, mod_ref):
    y = x * lax.rsqrt(jnp.mean(x * x, axis=-1, keepdims=True) + EPS)
    y = y * g_ref[...]
    return y * (1.0 + mod_ref[0, 1:2, :]) + mod_ref[0, 0:1, :]


def _qkv_kernel(x_ref, g_ref, mod_ref, w_ref, *refs, dils, tt, tc):
    o_refs, stage_ref = refs[:len(dils)], refs[len(dils)]
    n_slabs = x_ref.shape[-1] // LANES
    gw = QKV_GROUP_WIDTH

    def regroup(c):
        h = _prenorm_rows(x_ref[0, c * tc:(c + 1) * tc, :], g_ref, mod_ref)
        for cb in range(n_slabs):
            stage_ref[c % 2, cb] = h[:, cb * LANES:(cb + 1) * LANES]
        hs = []
        for d in dils:
            if d == 1:
                hs.append(h.astype(w_ref.dtype))
            else:
                hs.append(jnp.concatenate(
                    [jnp.concatenate([stage_ref[c % 2, cb, pl.ds(r, tc // d, stride=d), :]
                                      for cb in range(n_slabs)], axis=-1) for r in range(d)],
                    axis=0).astype(w_ref.dtype))
        return hs

    def project(c, hs):
        for gi, (d, h, o_ref) in enumerate(zip(dils, hs, o_refs)):
            res = jnp.dot(h, w_ref[:, gi * gw:(gi + 1) * gw], preferred_element_type=jnp.float32).astype(o_ref.dtype)
            rows = tc // d
            for r in range(d):
                o_ref[0, r, c * rows:(c + 1) * rows, :] = res[r * rows:(r + 1) * rows]

    h_next = regroup(0)
    for c in range(tt // tc):
        h_cur = h_next
        if (c + 1) * tc < tt:
            h_next = regroup(c + 1)
        project(c, h_cur)


def _qkv_proj(x, norm_g, mod3, w_qkv, dils):
    B, S, D = x.shape
    tt = 1024
    tc = 256
    kernel = functools.partial(_qkv_kernel, dils=dils, tt=tt, tc=tc)
    single = lambda shape: pl.BlockSpec(shape, lambda b, i: (0,) * len(shape), pipeline_mode=pl.Buffered(1))
    return pl.pallas_call(
        kernel,
        grid=(B, S // tt),
        in_specs=[pl.BlockSpec((1, tt, D), lambda b, i: (b, i, 0)),
                  single((1, D)),
                  pl.BlockSpec((1, 3, D), lambda b, i: (b, 0, 0)),
                  single(w_qkv.shape)],
        out_specs=[pl.BlockSpec((1, d, tt // d, QKV_GROUP_WIDTH), lambda b, i: (b, 0, i, 0)) for d in dils],
        out_shape=[jax.ShapeDtypeStruct((B, d, S // d, QKV_GROUP_WIDTH), jnp.bfloat16) for d in dils],
        scratch_shapes=[pltpu.VMEM((2, D // LANES, tc, LANES), jnp.float32)],
        compiler_params=pltpu.CompilerParams(vmem_limit_bytes=VMEM_LIMIT_BYTES),
        name="qkv_proj",
    )(x, norm_g.reshape(1, D), mod3, w_qkv)


def _attn_kernel(q_ref, kc_ref, kp_ref, vc_ref, vp_ref, bias_ref, o_ref, lse_ref, s_ref, m_ref, *, rows):
    n = pl.program_id(2)
    first_sel = jnp.where(n == 0, 1, 0)
    lane = lax.broadcasted_iota(jnp.int32, (1, LANES), 1)
    head_lanes = [lane < HEAD_DIM, lane >= HEAD_DIM]
    rhs_head = lax.broadcasted_iota(jnp.int32, (4 * QBLOCK, 2 * LANES), 0) // (2 * QBLOCK)
    rhs_lane_head = (lax.broadcasted_iota(jnp.int32, (4 * QBLOCK, 2 * LANES), 1) % LANES) // HEAD_DIM
    rhs_keep = rhs_head == rhs_lane_head
    n_heads = HEADS_PER_GROUP

    blocks_per_sub = rows // QBLOCK

    def pair_operands(ref_c, ref_p, rl, j, cs):
        if j == 0:
            return jnp.concatenate([ref_p[0, rl, :, cs], ref_c[0, rl, 0:QBLOCK, cs]], axis=0)
        return ref_c[0, rl, (j - 1) * QBLOCK:(j + 1) * QBLOCK, cs]

    def unit(u):
        blk, h = divmod(u, n_heads)
        rl, j = divmod(blk, blocks_per_sub)
        hp, e = divmod(h, 2)
        return rl, j, h, e, slice(hp * LANES, (hp + 1) * LANES)

    def logits_pass(u):
        rl, j, h, e, cs = unit(u)
        q_pair = q_ref[0, rl, j * QBLOCK:(j + 1) * QBLOCK, cs]
        k_pair = pair_operands(kc_ref, kp_ref, rl, j, cs)
        q_h = jnp.where(head_lanes[e], q_pair, jnp.zeros_like(q_pair))
        s = lax.dot_general(q_h, k_pair, (((1,), (1,)), ((), ())), preferred_element_type=jnp.float32)
        s = s + (bias_ref[first_sel, h] if j == 0 else bias_ref[0, h])
        s_ref[u % n_slots] = s
        m_ref[u % n_slots] = jnp.broadcast_to(jnp.max(s, axis=-1, keepdims=True), (QBLOCK, LANES))

    def values_pass(u0):
        rl, j, _, _, cs = unit(u0)
        v_pair = pair_operands(vc_ref, vp_ref, rl, j, cs)
        logits = [(s_ref[(u0 + e) % n_slots], m_ref[(u0 + e) % n_slots]) for e in range(2)]
        p_e = []
        for s, m in logits:
            p_e += [jnp.exp2(s[:, :LANES] - m), jnp.exp2(s[:, LANES:] - m)]
        p = jnp.concatenate(p_e, axis=-1).astype(v_pair.dtype)
        v_ones = jnp.concatenate([v_pair, jnp.ones_like(v_pair)], axis=-1)
        rhs = jnp.where(rhs_keep, jnp.concatenate([v_ones, v_ones], axis=0), jnp.zeros((), v_pair.dtype))
        acc = jnp.dot(p, rhs, preferred_element_type=jnp.float32)
        denom = acc[:, LANES:]
        qs = slice(j * QBLOCK, (j + 1) * QBLOCK)
        o_ref[0, rl, qs, cs] = (acc[:, :LANES] / denom).astype(o_ref.dtype)
        lse_ref[0, rl, qs, cs] = jnp.where(head_lanes[0], logits[0][1], logits[1][1]) * LN2 + jnp.log(denom)

    n_units = q_ref.shape[1] * blocks_per_sub * n_heads
    n_slots = s_ref.shape[0]
    lag = n_slots // 2
    for u in range(-lag, n_units, 2):
        if u + lag < n_units:
            logits_pass(u + lag)
            logits_pass(u + lag + 1)
        if u >= 0:
            values_pass(u)


def _attention(qkv, bias_tab):
    B, dil, L, _ = qkv.shape
    rows = min(ATTN_STEP_ROWS, L)
    nsub = min(dil, ATTN_STEP_ROWS // rows)
    rpb = rows // QBLOCK
    W = ATTN_WIDTH
    kernel = functools.partial(_attn_kernel, rows=rows)
    cur = lambda col: pl.BlockSpec((1, nsub, rows, W), lambda b, r, n: (b, r, n, col))
    prev = lambda col: pl.BlockSpec((1, nsub, QBLOCK, W),
                                    lambda b, r, n: (b, r, jnp.maximum(n * rpb - 1, 0), col))
    out_spec = pl.BlockSpec((1, nsub, rows, W), lambda b, r, n: (b, r, n, 0))
    return pl.pallas_call(
        kernel,
        grid=(B, dil // nsub, L // rows),
        in_specs=[cur(0), cur(1), prev(1), cur(2), prev(2),
                  pl.BlockSpec((2, HEADS_PER_GROUP, QBLOCK, 2 * QBLOCK), lambda b, r, n: (0, 0, 0, 0),
                               pipeline_mode=pl.Buffered(1))],
        out_specs=[out_spec, out_spec],
        out_shape=[jax.ShapeDtypeStruct((B, dil, L, W), jnp.bfloat16),
                   jax.ShapeDtypeStruct((B, dil, L, W), jnp.float32)],
        scratch_shapes=[pltpu.VMEM((2 * ATTN_PIPELINE_LAG, QBLOCK, 2 * QBLOCK), jnp.float32),
                        pltpu.VMEM((2 * ATTN_PIPELINE_LAG, QBLOCK, LANES), jnp.float32)],
        compiler_params=pltpu.CompilerParams(vmem_limit_bytes=VMEM_LIMIT_BYTES),
        name=f"band_attn_d{dil}",
    )(qkv, qkv, qkv, qkv, qkv, bias_tab)


def _sigmoid(v):
    return 1.0 / (1.0 + jnp.exp(-v))


OUT_ROW_CHUNK = 128
OUT_MM_ROWS = 256
EXT_HALO = 2 * POOL_HALO


def _out_kernel(x_ref, xprev_ref, g_ref, mod_ref,
                o0_ref, o1_ref, o2_ref, l0_ref, l1_ref, l2_ref,
                w_rest_ref, pool_w_ref, pool_scale_ref, w_attn_ref, w_pool_ref, w_out_ref, fin_ref,
                out_ref,
                h_ref, rest_ref, ext_ref, ptmp_ref, mixed_ref, ga_ref, gp_ref, ya_ref, yp_ref, mg_ref,
                so1_ref, so2_ref, sl1_ref, sl2_ref, *, tm, apply_final_norm):
    i = pl.program_id(1)
    W = ATTN_WIDTH
    bf16 = jnp.bfloat16
    n_slabs = W // LANES
    rows = lambda c: slice(c * OUT_ROW_CHUNK, (c + 1) * OUT_ROW_CHUNK)
    half = lambda a: slice(a * OUT_MM_ROWS, (a + 1) * OUT_MM_ROWS)
    Z_ATTN, U_POOL, Z_POOL, G_ATTN, G_POOL = (0, W), (W, 2 * W), (2 * W, 3 * W), (3 * W, 5 * W), (5 * W, 7 * W)
    silu = lambda z: z * _sigmoid(z)

    def v_pre(c):
        h_ref[rows(c), :] = _prenorm_rows(x_ref[0, rows(c), :], g_ref, mod_ref).astype(bf16)

    def m_proj(a, cols, lo_hi=None):
        lo, hi = lo_hi or cols
        res = jnp.dot(h_ref[half(a), :], w_rest_ref[:, lo:hi], preferred_element_type=jnp.float32)
        if cols == U_POOL:
            ext_ref[EXT_HALO + a * OUT_MM_ROWS:EXT_HALO + (a + 1) * OUT_MM_ROWS, :] = res
        else:
            rest_ref[half(a), lo:hi] = res

    def m_uprev():
        h_prev = _prenorm_rows(xprev_ref[0], g_ref, mod_ref).astype(bf16)
        u_prev = jnp.dot(h_prev, w_rest_ref[:, W:2 * W], preferred_element_type=jnp.float32)
        ext_ref[0:EXT_HALO - POOL_HALO, :] = jnp.zeros((EXT_HALO - POOL_HALO, W), jnp.float32)
        ext_ref[EXT_HALO - POOL_HALO:EXT_HALO, :] = jnp.where(i == 0, 0.0, u_prev)

    def v_tok(ref, stage_ref, r_lo, r_hi):
        d = ref.shape[1]
        for r in range(r_lo, r_hi):
            for cb in range(n_slabs):
                stage_ref[cb, pl.ds(r, tm // d, stride=d), :] = (
                    ref[0, r, :, cb * LANES:(cb + 1) * LANES].astype(jnp.float32))

    def staged(stage_ref, c):
        return jnp.concatenate([stage_ref[cb, rows(c), :] for cb in range(n_slabs)], axis=-1)

    def v_merge(c):
        l0, l1, l2 = l0_ref[0, 0, rows(c), :], staged(sl1_ref, c), staged(sl2_ref, c)
        mx = jnp.maximum(jnp.maximum(l0, l1), l2)
        e0, e1, e2 = jnp.exp(l0 - mx), jnp.exp(l1 - mx), jnp.exp(l2 - mx)
        num = (e0 * o0_ref[0, 0, rows(c), :].astype(jnp.float32) + e1 * staged(so1_ref, c)
               + e2 * staged(so2_ref, c))
        attn = num / (e0 + e1 + e2)
        ga_ref[rows(c), :] = (attn * silu(rest_ref[rows(c), Z_ATTN[0]:Z_ATTN[1]])).astype(bf16)

    def pool(g, a):
        win = POOL_WINDOWS[g]
        cs = slice(g * LANES, (g + 1) * LANES)
        base = EXT_HALO + a * OUT_MM_ROWS
        u = ext_ref[base:base + OUT_MM_ROWS, cs]
        levels = win.bit_length() - 1
        top = EXT_HALO + OUT_MM_ROWS
        s = None
        for lvl in range(1, levels + 1):
            shift, lo = 1 << (lvl - 1), EXT_HALO - 8 * (levels - lvl)
            if lvl == 1:
                r0 = base - EXT_HALO
                s = ext_ref[r0 + lo:r0 + top, cs] + ext_ref[r0 + lo - shift:r0 + top - shift, cs]
            else:
                s = ptmp_ref[g, a, lvl - 2, lo:top, :] + ptmp_ref[g, a, lvl - 2, lo - shift:top - shift, :]
            if lvl < levels:
                ptmp_ref[g, a, lvl - 1, lo:top, :] = s
        t = i * tm + a * OUT_MM_ROWS + lax.broadcasted_iota(jnp.int32, (OUT_MM_ROWS, 1), 0)
        cnt = jnp.minimum(t + 1, win).astype(jnp.float32)
        pooled = s / cnt - u
        mixed_ref[half(a), cs] = jnp.dot(pooled.astype(bf16), pool_w_ref[g],
                                         preferred_element_type=jnp.float32) * pool_scale_ref[:, cs]

    def v_gate_p(c):
        gp_ref[rows(c), :] = (mixed_ref[rows(c), :] * silu(rest_ref[rows(c), Z_POOL[0]:Z_POOL[1]])).astype(bf16)

    def m_yp(a):
        yp_ref[half(a), :] = jnp.dot(gp_ref[half(a), :], w_pool_ref[...], preferred_element_type=jnp.float32)

    def m_ya(a):
        ya_ref[half(a), :] = jnp.dot(ga_ref[half(a), :], w_attn_ref[...], preferred_element_type=jnp.float32)

    def v_sig(c):
        mg_ref[rows(c), :] = (_sigmoid(rest_ref[rows(c), G_ATTN[0]:G_ATTN[1]]) * ya_ref[rows(c), :]
                              + _sigmoid(rest_ref[rows(c), G_POOL[0]:G_POOL[1]]) * yp_ref[rows(c), :]).astype(bf16)

    def m_out(a):
        out_ref[0, half(a), :] = jnp.dot(mg_ref[half(a), :], w_out_ref[...], preferred_element_type=jnp.float32)

    def v_fin(c):
        y = x_ref[0, rows(c), :] + mod_ref[0, 2:3, :] * out_ref[0, rows(c), :]
        if apply_final_norm:
            y = y * lax.rsqrt(jnp.mean(y * y, axis=-1, keepdims=True) + EPS) * fin_ref[...]
        out_ref[0, rows(c), :] = y

    d1, d2 = l1_ref.shape[1], l2_ref.shape[1]
    q2 = d2 // 4
    g_halves = lambda cols: [(cols[0], cols[0] + W), (cols[0] + W, cols[1])]
    v_pre(0); v_pre(1); m_uprev()
    m_proj(0, U_POOL); v_pre(2)
    m_proj(0, Z_POOL); v_pre(3)
    m_proj(1, U_POOL); v_tok(l1_ref, sl1_ref, 0, d1)
    m_proj(1, Z_POOL); v_tok(o1_ref, so1_ref, 0, d1)
    m_proj(0, Z_ATTN); pool(0, 0); pool(0, 1); pool(1, 0)
    m_proj(1, Z_ATTN); pool(1, 1); pool(2, 0)
    m_proj(0, G_POOL, g_halves(G_POOL)[0]); pool(2, 1)
    m_proj(0, G_POOL, g_halves(G_POOL)[1]); pool(3, 0)
    m_proj(1, G_POOL, g_halves(G_POOL)[0]); pool(3, 1)
    m_proj(1, G_POOL, g_halves(G_POOL)[1]); v_tok(l2_ref, sl2_ref, 0, q2); v_tok(l2_ref, sl2_ref, q2, 2 * q2)
    m_proj(0, G_ATTN, g_halves(G_ATTN)[0]); v_tok(l2_ref, sl2_ref, 2 * q2, 3 * q2); v_tok(l2_ref, sl2_ref, 3 * q2, d2)
    m_proj(0, G_ATTN, g_halves(G_ATTN)[1]); v_tok(o2_ref, so2_ref, 0, 2 * q2)
    m_proj(1, G_ATTN, g_halves(G_ATTN)[0]); v_tok(o2_ref, so2_ref, 2 * q2, d2)
    m_proj(1, G_ATTN, g_halves(G_ATTN)[1]); v_gate_p(0); v_gate_p(1); v_merge(0)
    m_yp(0); v_gate_p(2); v_gate_p(3); v_merge(1)
    m_yp(1); v_merge(2); v_merge(3)
    m_ya(0); v_sig(0)
    m_ya(1); v_sig(1)
    v_sig(2); v_sig(3)
    m_out(0); v_fin(0)
    m_out(1); v_fin(1)
    v_fin(2); v_fin(3)


def _merge_out(x, norm_g, mod3, attn_parts, w_rest, pool_w, pool_scale, w_attn_br, w_pool_br, w_out,
               final_g, apply_final_norm):
    B, S, D = x.shape
    tm = 512
    W = ATTN_WIDTH
    hb = tm // POOL_HALO
    tok = lambda width: pl.BlockSpec((1, tm, width), lambda b, i: (b, i, 0))
    const = lambda shape: pl.BlockSpec(shape, lambda b, i: (0,) * len(shape), pipeline_mode=pl.Buffered(1))
    (o0, l0), (o1, l1), (o2, l2) = attn_parts
    sub = lambda a: pl.BlockSpec((1, a.shape[1], tm // a.shape[1], W), lambda b, i: (b, 0, i, 0))
    stage = pltpu.VMEM((W // LANES, tm, LANES), jnp.float32)
    kernel = functools.partial(_out_kernel, tm=tm, apply_final_norm=apply_final_norm)
    return pl.pallas_call(
        kernel,
        grid=(B, S // tm),
        in_specs=[tok(D),
                  pl.BlockSpec((1, POOL_HALO, D), lambda b, i: (b, jnp.maximum(i * hb - 1, 0), 0)),
                  const((1, D)),
                  pl.BlockSpec((1, 3, D), lambda b, i: (b, 0, 0)),
                  sub(o0), sub(o1), sub(o2), sub(l0), sub(l1), sub(l2),
                  const(w_rest.shape), const(pool_w.shape), const((1, POOL_GROUPS * LANES)),
                  const(w_attn_br.shape), const(w_pool_br.shape), const(w_out.shape), const((1, D))],
        out_specs=tok(D),
        out_shape=jax.ShapeDtypeStruct((B, S, D), x.dtype),
        scratch_shapes=[pltpu.VMEM((tm, D), jnp.bfloat16),
                        pltpu.VMEM((tm, w_rest.shape[1]), jnp.float32),
                        pltpu.VMEM((tm + EXT_HALO, POOL_GROUPS * LANES), jnp.float32),
                        pltpu.VMEM((POOL_GROUPS, tm // OUT_MM_ROWS, 3, EXT_HALO + OUT_MM_ROWS, LANES),
                                   jnp.float32),
                        pltpu.VMEM((tm, W), jnp.float32),
                        pltpu.VMEM((tm, W), jnp.bfloat16),
                        pltpu.VMEM((tm, W), jnp.bfloat16),
                        pltpu.VMEM((tm, D), jnp.float32),
                        pltpu.VMEM((tm, D), jnp.float32),
                        pltpu.VMEM((tm, D), jnp.bfloat16),
                        stage, stage, stage, stage],
        compiler_params=pltpu.CompilerParams(vmem_limit_bytes=VMEM_LIMIT_BYTES),
        name="merge_out",
    )(x, x, norm_g.reshape(1, D), mod3, o0, o1, o2, l0, l1, l2,
      w_rest, pool_w, pool_scale.reshape(1, -1), w_attn_br, w_pool_br, w_out, final_g.reshape(1, D))


def kernel(x, c, norm_g, w_ada, b_ada, w_in, pool_w, pool_scale, w_attn_br, w_pool_br, w_out, rel_bias, final_g):
    B, S, D = x.shape
    depth = norm_g.shape[0]
    bf16 = jnp.bfloat16
    col = np.arange(QKV_WIDTH) % QKV_GROUP_WIDTH
    q_scale = jnp.asarray(np.where(col < ATTN_WIDTH, HEAD_DIM ** -0.5 * LOG2E, 1.0), jnp.float32)
    tables = []
    for gi, (win, dil) in enumerate(ATTN_GROUPS):
        assert win // dil <= QBLOCK and S % (dil * QBLOCK) == 0
        tables.append(_bias_tables(rel_bias[:, gi * HEADS_PER_GROUP:(gi + 1) * HEADS_PER_GROUP], dil, win // dil))

    for l in range(depth):
        w_qkv = (w_in[l, :, :QKV_WIDTH] * q_scale).astype(bf16)
        w_rest = w_in[l, :, QKV_WIDTH:].astype(bf16)
        mod3 = _modulation(c, w_ada[l], b_ada[l]).reshape(B, 3, D)
        qkvs = _qkv_proj(x, norm_g[l], mod3, w_qkv, tuple(dil for _, dil in ATTN_GROUPS))
        attn_parts = [_attention(qkv, table) for qkv, table in zip(qkvs, tables)]
        x = _merge_out(x, norm_g[l], mod3, attn_parts, w_rest, pool_w[l].astype(bf16), pool_scale[l],
                       w_attn_br[l].astype(bf16), w_pool_br[l].astype(bf16), w_out[l].astype(bf16),
                       final_g, apply_final_norm=(l == depth - 1))
    return x
```
